```python
import math
import jax, jax.numpy as jnp
from jax import lax
import numpy as np

D_MODEL = 1024
BATCH = 16
SEQ = 4096
DEPTH = 4

CHUNK = 64
N_MIXERS = 2
N_MAMBA = (DEPTH + 1) // 2
N_RWKV = DEPTH // 2
N_VRES = max(N_RWKV - 1, 0)

D_FF = 2816
FFN_HALF = 0.5
NORM_EPS = 1e-6
N_NORMS = 6

M_EXPAND = 2
M_D_INNER = M_EXPAND * D_MODEL
M_HEADDIM = 64
M_HEADS = M_D_INNER // M_HEADDIM
M_GROUPS = 8
M_HPG = M_HEADS // M_GROUPS
M_D_STATE = 128
M_D_CONV = 4
M_CONV_DIM = M_D_INNER + 2 * M_GROUPS * M_D_STATE
M_D_IN_PROJ = M_D_INNER + M_CONV_DIM + M_HEADS
M_DT_MIN = 0.001
M_DT_MAX = 0.1
M_NORM_EPS = 1e-5

R_HEAD = 64
R_HEADS = D_MODEL // R_HEAD
R_DECAY_LORA = 64
R_AAA_LORA = 64
R_MV_LORA = 32
R_GATE_LORA = 128
R_N_SHIFT = 6
R_LNX_EPS = 64e-5
R_L2_EPS = 1e-12

kernel_name = "hybrid_mamba2_rwkv7_macaron_sandwich"


def rmsnorm(x, g, eps=NORM_EPS):
    xf = x.astype(jnp.float32)
    y = xf * lax.rsqrt(jnp.mean(xf * xf, axis=-1, keepdims=True) + eps)
    return (y * g.astype(jnp.float32)).astype(x.dtype)


def swiglu_ffn(x, w_in, w_out):
    gate, up = jnp.split(x @ w_in, 2, axis=-1)
    return (jax.nn.silu(gate) * up) @ w_out


def causal_depthwise_conv(x, w, b):
    k = w.shape[0]
    out = lax.conv_general_dilated(
        x, w[:, None, :].astype(x.dtype), window_strides=(1,), padding=[(k - 1, 0)],
        dimension_numbers=('NWC', 'WIO', 'NWC'), feature_group_count=x.shape[-1])
    return out + b.astype(x.dtype)


def ssd_chunked_scan(xh, dt, A, Bm, Cm):
    b, l, g, e, p = xh.shape
    n = Bm.shape[-1]
    nc = l // CHUNK

    def to_chunks(t):
        t = t.astype(jnp.float32).reshape((b, nc, CHUNK) + t.shape[2:])
        return jnp.moveaxis(t, 1, 0)

    xs = (to_chunks(xh), to_chunks(dt), to_chunks(Bm), to_chunks(Cm))
    A = A.astype(jnp.float32)
    causal = jnp.tril(jnp.ones((CHUNK, CHUNK), dtype=bool))[None, :, :, None, None]

    def step(S, inp):
        x_c, dt_c, B_c, C_c = inp
        a_cum = jnp.cumsum(dt_c * A, axis=1)
        seg = a_cum[:, :, None] - a_cum[:, None, :]
        decay = jnp.exp(jnp.where(causal, seg, -jnp.inf))
        cb = jnp.einsum('bign,bjgn->bijg', C_c, B_c)
        w_ij = cb[..., None] * decay * dt_c[:, None]
        y = jnp.einsum('bijge,bjgep->bigep', w_ij, x_c)
        y = y + jnp.einsum('bign,bgepn->bigep', C_c, S) * jnp.exp(a_cum)[..., None]
        to_end = jnp.exp(a_cum[:, -1:] - a_cum) * dt_c
        S = S * jnp.exp(a_cum[:, -1])[..., None, None] + jnp.einsum(
            'blge,blgn,blgep->bgepn', to_end, B_c, x_c)
        return S, y

    S0 = jnp.zeros((b, g, e, p, n), jnp.float32)
    _, ys = lax.scan(step, S0, xs)
    return jnp.moveaxis(ys, 0, 1).reshape(b, l, g, e, p)


def mamba2_mixer(u, in_proj, conv_w, conv_b, dt_bias, A_log, D_skip, norm_w, out_proj):
    f32 = jnp.float32
    b, l, _ = u.shape
    z, xbc, dt = jnp.split(u @ in_proj, [M_D_INNER, M_D_INNER + M_CONV_DIM], axis=-1)
    xbc = jax.nn.silu(causal_depthwise_conv(xbc, conv_w, conv_b))
    xs, Bm, Cm = jnp.split(xbc, [M_D_INNER, M_D_INNER + M_GROUPS * M_D_STATE], axis=-1)
    xh = xs.reshape(b, l, M_GROUPS, M_HPG, M_HEADDIM)
    Bm = Bm.reshape(b, l, M_GROUPS, M_D_STATE)
    Cm = Cm.reshape(b, l, M_GROUPS, M_D_STATE)
    dt = jax.nn.softplus(dt.astype(f32) + dt_bias.astype(f32)).reshape(b, l, M_GROUPS, M_HPG)
    A = -jnp.exp(A_log.astype(f32)).reshape(M_GROUPS, M_HPG)
    y = ssd_chunked_scan(xh, dt, A, Bm, Cm)
    y = y + xh.astype(f32) * D_skip.astype(f32).reshape(M_GROUPS, M_HPG, 1)
    y = y.reshape(b, l, M_D_INNER) * jax.nn.silu(z.astype(f32))
    yg = y.reshape(b, l, M_GROUPS, M_D_INNER // M_GROUPS)
    yg = yg * lax.rsqrt(jnp.mean(yg * yg, axis=-1, keepdims=True) + M_NORM_EPS)
    y = yg.reshape(b, l, M_D_INNER) * norm_w.astype(f32)
    return y.astype(u.dtype) @ out_proj


def rwkv7_recurrence(r, w, k, v, kk, alpha):
    b, l, h, n = r.shape

    def step(S, inp):
        r_t, w_t, k_t, v_t, kk_t, a_t = inp
        sa = jnp.einsum('bhvk,bhk->bhv', S, -kk_t)
        S = (S * w_t[:, :, None, :] + sa[..., None] * (kk_t * a_t)[:, :, None, :]
             + v_t[..., None] * k_t[:, :, None, :])
        return S, jnp.einsum('bhvk,bhk->bhv', S, r_t)

    xs = tuple(jnp.moveaxis(t, 1, 0) for t in (r, w, k, v, kk, alpha))
    S0 = jnp.zeros((b, h, n, n), jnp.float32)
    _, ys = lax.scan(step, S0, xs)
    return jnp.moveaxis(ys, 0, 1)


def rwkv7_mixer(u, v_first, mix, w_rkv, w_o, w0, w1, w2, a0, a1, a2, g1, g2,
                k_k, k_a, r_k, lnx_w, lnx_b, vres):
    f32 = jnp.float32
    b, l, d = u.shape
    delta = jnp.pad(u, ((0, 0), (1, 0), (0, 0)))[:, :-1] - u
    shift = lambda idx: u + delta * mix[idx]
    x_v = shift(3)
    r = shift(0) @ w_rkv[0]
    k = shift(2) @ w_rkv[1]
    v = x_v @ w_rkv[2]
    w_log = -jax.nn.softplus(-(w0 + jnp.tanh(shift(1) @ w1) @ w2).astype(f32)) - 0.5
    decay = jnp.exp(-jnp.exp(w_log))
    if vres is not None:
        v0, v1, v2 = vres
        v = v + (v_first - v) * jax.nn.sigmoid(v0 + (x_v @ v1) @ v2)
    alpha = jax.nn.sigmoid((a0 + (shift(4) @ a1) @ a2).astype(f32))
    g = jax.nn.sigmoid(shift(5) @ g1) @ g2

    heads = lambda t: t.astype(f32).reshape(b, l, R_HEADS, R_HEAD)
    kk = heads(k * k_k)
    kk = kk / jnp.maximum(jnp.sqrt(jnp.sum(kk * kk, axis=-1, keepdims=True)), R_L2_EPS)
    alpha_h = heads(alpha)
    k_h = heads(k) * (1.0 + (alpha_h - 1.0) * k_a.astype(f32).reshape(R_HEADS, R_HEAD))
    r_h, v_h = heads(r), heads(v)
    y = rwkv7_recurrence(r_h, heads(decay), k_h, v_h, kk, alpha_h)
    mu = jnp.mean(y, axis=-1, keepdims=True)
    var = jnp.mean(jnp.square(y - mu), axis=-1, keepdims=True)
    y = ((y - mu) * lax.rsqrt(var + R_LNX_EPS) * lnx_w.astype(f32).reshape(R_HEADS, R_HEAD)
         + lnx_b.astype(f32).reshape(R_HEADS, R_HEAD))
    y = y + jnp.sum(r_h * k_h * r_k.astype(f32), axis=-1, keepdims=True) * v_h
    y = y.reshape(b, l, d).astype(u.dtype)
    return (y * g) @ w_o, v


def setup_inputs(seed: int = 0) -> dict:
    key = jax.random.key(seed)
    ks = iter(jax.random.split(key, 40))
    nrm = lambda shape, scale: scale * jax.random.normal(next(ks), shape, jnp.float32)
    uni = lambda shape, lo, hi: jax.random.uniform(next(ks), shape, jnp.float32, lo, hi)
    D = D_MODEL
    x = nrm((BATCH, SEQ, D), 1.0)
    norms = 1.0 + nrm((DEPTH, N_NORMS, D), 0.02)
    ffn_w_in = nrm((DEPTH, 2, D, 2 * D_FF), D ** -0.5)
    ffn_w_out = nrm((DEPTH, 2, D_FF, D), D_FF ** -0.5)
    m_in_proj = nrm((N_MAMBA, D, M_D_IN_PROJ), D ** -0.5)
    m_conv_w = nrm((N_MAMBA, M_D_CONV, M_CONV_DIM), M_D_CONV ** -0.5)
    m_conv_b = nrm((N_MAMBA, M_CONV_DIM), 0.02)
    dt0 = jnp.maximum(jnp.exp(uni((N_MAMBA, M_HEADS), math.log(M_DT_MIN), math.log(M_DT_MAX))), 1e-4)
    m_dt_bias = dt0 + jnp.log(-jnp.expm1(-dt0))
    m_A_log = jnp.log(uni((N_MAMBA, M_HEADS), 1.0, 16.0))
    m_D = 1.0 + nrm((N_MAMBA, M_HEADS), 0.1)
    m_norm_w = 1.0 + nrm((N_MAMBA, M_D_INNER), 0.02)
    m_out_proj = nrm((N_MAMBA, M_D_INNER, D), M_D_INNER ** -0.5)
    r_mix = uni((N_RWKV, R_N_SHIFT, D), 0.0, 1.0)
    r_w_rkv = nrm((N_RWKV, 3, D, D), D ** -0.5)
    r_w_o = nrm((N_RWKV, D, D), D ** -0.5)
    r_w0 = uni((N_RWKV, D), -6.0, 0.0)
    r_w1 = nrm((N_RWKV, D, R_DECAY_LORA), D ** -0.5)
    r_w2 = nrm((N_RWKV, R_DECAY_LORA, D), 0.1 * R_DECAY_LORA ** -0.5)
    r_a0 = nrm((N_RWKV, D), 0.1)
    r_a1 = nrm((N_RWKV, D, R_AAA_LORA), D ** -0.5)
    r_a2 = nrm((N_RWKV, R_AAA_LORA, D), 0.1 * R_AAA_LORA ** -0.5)
    r_v0 = 1.0 + nrm((N_VRES, D), 0.1)
    r_v1 = nrm((N_VRES, D, R_MV_LORA), D ** -0.5)
    r_v2 = nrm((N_VRES, R_MV_LORA, D), 0.1 * R_MV_LORA ** -0.5)
    r_g1 = nrm((N_RWKV, D, R_GATE_LORA), D ** -0.5)
    r_g2 = nrm((N_RWKV, R_GATE_LORA, D), R_GATE_LORA ** -0.5)
    r_k_k = 0.85 + nrm((N_RWKV, D), 0.02)
    r_k_a = 1.0 + nrm((N_RWKV, D), 0.02)
    r_r_k = nrm((N_RWKV, R_HEADS, R_HEAD), 0.1)
    r_lnx_w = 1.0 + nrm((N_RWKV, D), 0.02)
    r_lnx_b = nrm((N_RWKV, D), 0.02)
    return {"x": x, "norms": norms, "ffn_w_in": ffn_w_in, "ffn_w_out": ffn_w_out,
            "m_in_proj": m_in_proj, "m_conv_w": m_conv_w, "m_conv_b": m_conv_b,
            "m_dt_bias": m_dt_bias, "m_A_log": m_A_log, "m_D": m_D, "m_norm_w": m_norm_w,
            "m_out_proj": m_out_proj,
            "r_mix": r_mix, "r_w_rkv": r_w_rkv, "r_w_o": r_w_o, "r_w0": r_w0, "r_w1": r_w1,
            "r_w2": r_w2, "r_a0": r_a0, "r_a1": r_a1, "r_a2": r_a2, "r_v0": r_v0, "r_v1": r_v1,
            "r_v2": r_v2, "r_g1": r_g1, "r_g2": r_g2, "r_k_k": r_k_k, "r_k_a": r_k_a,
            "r_r_k": r_r_k, "r_lnx_w": r_lnx_w, "r_lnx_b": r_lnx_b}


def reference(x, norms, ffn_w_in, ffn_w_out,
              m_in_proj, m_conv_w, m_conv_b, m_dt_bias, m_A_log, m_D, m_norm_w, m_out_proj,
              r_mix, r_w_rkv, r_w_o, r_w0, r_w1, r_w2, r_a0, r_a1, r_a2, r_v0, r_v1, r_v2,
              r_g1, r_g2, r_k_k, r_k_a, r_r_k, r_lnx_w, r_lnx_b):
    h = x
    v_first = None
    for i in range(DEPTH):
        t = swiglu_ffn(rmsnorm(h, norms[i, 0]), ffn_w_in[i, 0], ffn_w_out[i, 0])
        h = h + FFN_HALF * rmsnorm(t, norms[i, 1])
        u = rmsnorm(h, norms[i, 2])
        j = i // N_MIXERS
        if i % N_MIXERS == 0:
            t = mamba2_mixer(u, m_in_proj[j], m_conv_w[j], m_conv_b[j], m_dt_bias[j],
                             m_A_log[j], m_D[j], m_norm_w[j], m_out_proj[j])
        else:
            vres = (r_v0[j - 1], r_v1[j - 1], r_v2[j - 1]) if j > 0 else None
            t, v_layer = rwkv7_mixer(u, v_first, r_mix[j], r_w_rkv[j], r_w_o[j], r_w0[j], r_w1[j],
                                     r_w2[j], r_a0[j], r_a1[j], r_a2[j], r_g1[j], r_g2[j],
                                     r_k_k[j], r_k_a[j], r_r_k[j], r_lnx_w[j], r_lnx_b[j], vres)
            if j == 0:
                v_first = v_layer
        h = h + rmsnorm(t, norms[i, 3])
        t = swiglu_ffn(rmsnorm(h, norms[i, 4]), ffn_w_in[i, 1], ffn_w_out[i, 1])
        h = h + FFN_HALF * rmsnorm(t, norms[i, 5])
    return h
```

```python
import functools

import numpy as np
import jax
import jax.numpy as jnp
from jax import lax
from jax.experimental import pallas as pl
from jax.experimental.pallas import tpu as pltpu

F32 = jnp.float32
BF16 = jnp.bfloat16

NORM_EPS = 1e-6
N_NORMS = 6
FFN_HALF = 0.5

M_HEADDIM = 64
M_GROUPS = 8
M_HPG = 4
M_D_STATE = 128
M_D_CONV = 4
M_NORM_EPS = 1e-5
M_GW = M_HPG * M_HEADDIM
M_DT_PAD = 128
SSD_Q = 128

R_HEAD = 64
R_LNX_EPS = 64e-5
R_L2_EPS = 1e-12
R_LORA_PAD = 128
R_CHUNK = 64
R_PAIR = 2 * R_HEAD
R_BLK = 256

VMEM_LIMIT = 56 * 1024 * 1024


def _rmsnorm(x, g, eps=NORM_EPS):
    return x * lax.rsqrt(jnp.mean(x * x, axis=-1, keepdims=True) + eps) * g


def _sigmoid(x):
    return 1.0 / (1.0 + jnp.exp(-x))


def _silu(x):
    return x * _sigmoid(x)


def _softplus(x):
    return jnp.maximum(x, 0.0) + jnp.log(1.0 + jnp.exp(-jnp.abs(x)))


def _dot(a, b):
    return jnp.dot(a, b, preferred_element_type=F32)


def _dot_nt(a, b):
    return lax.dot_general(a, b, (((1,), (1,)), ((), ())), preferred_element_type=F32)


def _dot_tn(a, b):
    return lax.dot_general(a, b, (((0,), (0,)), ((), ())), preferred_element_type=F32)


def _split(x):
    hi = x.astype(BF16)
    lo = (x - hi.astype(F32)).astype(BF16)
    return hi, lo


def _dot_x_exact(x, m):
    hi, lo = _split(x)
    return _dot(hi, m) + _dot(lo, m)


def _dot_exact_x(m, x):
    hi, lo = _split(x)
    return _dot(m, hi) + _dot(m, lo)


def _params(sem):
    return pltpu.CompilerParams(dimension_semantics=sem, vmem_limit_bytes=VMEM_LIMIT)


def _resident(block, index_map):
    return pl.BlockSpec(block, index_map, pipeline_mode=pl.Buffered(1))


def _ffn_kernel(x_ref, gpre_ref, gpost_ref, win_ref, wout_ref, o_ref, acc_ref, *, d_ff, cf):
    x = x_ref[...]
    xn = _rmsnorm(x, gpre_ref[...]).astype(BF16)
    for c in range(d_ff // cf):
        gate = _dot(xn, win_ref[:, c * cf:(c + 1) * cf])
        up = _dot(xn, win_ref[:, d_ff + c * cf:d_ff + (c + 1) * cf])
        act = (_silu(gate) * up).astype(BF16)
        part = _dot(act, wout_ref[c * cf:(c + 1) * cf, :])
        if c == 0:
            acc_ref[...] = part
        else:
            acc_ref[...] += part
    o_ref[...] = x + FFN_HALF * _rmsnorm(acc_ref[...], gpost_ref[...])


def _ffn(h, norms3, w_in, w_out, layer, which, tm):
    t, d = h.shape
    d_ff = w_out.shape[2]
    n0 = layer * N_NORMS + (0 if which == 0 else 4)
    kern = functools.partial(_ffn_kernel, d_ff=d_ff, cf=256)
    return pl.pallas_call(
        kern,
        out_shape=jax.ShapeDtypeStruct((t, d), F32),
        grid=(t // tm,),
        in_specs=[
            pl.BlockSpec((tm, d), lambda i: (i, 0)),
            pl.BlockSpec((None, 1, d), lambda i: (n0, 0, 0)),
            pl.BlockSpec((None, 1, d), lambda i: (n0 + 1, 0, 0)),
            _resident((None, None, d, 2 * d_ff), lambda i: (layer, which, 0, 0)),
            _resident((None, None, d_ff, d), lambda i: (layer, which, 0, 0)),
        ],
        out_specs=pl.BlockSpec((tm, d), lambda i: (i, 0)),
        scratch_shapes=[pltpu.VMEM((tm, d), F32)],
        compiler_params=_params(("parallel",)),
        name="ffn",
    )(h, norms3, norms3, w_in, w_out)


def _in_proj_kernel(h_ref, g_ref, w_ref, z_ref, xbc_ref, dt_ref, *, d_inner, conv_dim, cn):
    u = _rmsnorm(h_ref[...], g_ref[...]).astype(BF16)
    for c in range(d_inner // cn):
        z_ref[:, c * cn:(c + 1) * cn] = _dot(u, w_ref[:, c * cn:(c + 1) * cn])
    for c in range(conv_dim // cn):
        xbc_ref[:, c * cn:(c + 1) * cn] = _dot(u, w_ref[:, d_inner + c * cn:d_inner + (c + 1) * cn])
    dt_ref[...] = _dot(u, w_ref[:, d_inner + conv_dim:])


def _mamba_in_proj(h, norms3, w, layer, j, d_inner, conv_dim, tm):
    t, d = h.shape
    n_all = w.shape[2]
    kern = functools.partial(_in_proj_kernel, d_inner=d_inner, conv_dim=conv_dim, cn=512)
    return pl.pallas_call(
        kern,
        out_shape=(jax.ShapeDtypeStruct((t, d_inner), F32),
                   jax.ShapeDtypeStruct((t, conv_dim), F32),
                   jax.ShapeDtypeStruct((t, M_DT_PAD), F32)),
        grid=(t // tm,),
        in_specs=[
            pl.BlockSpec((tm, d), lambda i: (i, 0)),
            pl.BlockSpec((None, 1, d), lambda i: (layer * N_NORMS + 2, 0, 0)),
            _resident((None, d, n_all), lambda i: (j, 0, 0)),
        ],
        out_specs=(pl.BlockSpec((tm, d_inner), lambda i: (i, 0)),
                   pl.BlockSpec((tm, conv_dim), lambda i: (i, 0)),
                   pl.BlockSpec((tm, M_DT_PAD), lambda i: (i, 0))),
        compiler_params=_params(("parallel",)),
        name="mamba_in_proj",
    )(h, norms3, w)


def _out_proj_kernel(y_ref, w_ref, h_ref, g_ref, o_ref):
    t = _dot(y_ref[...], w_ref[...])
    o_ref[...] = h_ref[...] + _rmsnorm(t, g_ref[...])


def _out_proj(y, w, j, h, norms3, layer, tm):
    t, d = h.shape
    k = y.shape[1]
    return pl.pallas_call(
        _out_proj_kernel,
        out_shape=jax.ShapeDtypeStruct((t, d), F32),
        grid=(t // tm,),
        in_specs=[
            pl.BlockSpec((tm, k), lambda i: (i, 0)),
            _resident((None, k, d), lambda i: (j, 0, 0)),
            pl.BlockSpec((tm, d), lambda i: (i, 0)),
            pl.BlockSpec((None, 1, d), lambda i: (layer * N_NORMS + 3, 0, 0)),
        ],
        out_specs=pl.BlockSpec((tm, d), lambda i: (i, 0)),
        compiler_params=_params(("parallel",)),
        name="out_proj",
    )(y, w, h, norms3)


def _ssd_kernel(z_ref, xbc_ref, dt_ref, cw_ref, cb_ref, dtb_ref, alog_ref, dskip_ref, nw_ref,
                yn_ref, xpad_ref, st_ref, *, q, d_inner):
    c = pl.program_id(1)
    gn = M_GROUPS * M_D_STATE

    @pl.when(c == 0)
    def _():
        xpad_ref[0:8, :] = jnp.zeros((8, xpad_ref.shape[1]), F32)
        st_ref[...] = jnp.zeros_like(st_ref)

    @pl.when(c > 0)
    def _():
        xpad_ref[0:8, :] = xpad_ref[q:q + 8, :]

    xpad_ref[8:q + 8, :] = xbc_ref[...]

    row = lax.broadcasted_iota(jnp.int32, (q, q), 0)
    col = lax.broadcasted_iota(jnp.int32, (q, q), 1)
    causal = row >= col
    tril = jnp.where(causal, 1.0, 0.0).astype(BF16)

    dtv = _softplus(dt_ref[...] + dtb_ref[...])
    dta = dtv * (-jnp.exp(alog_ref[...]))
    a_cum = _dot_exact_x(tril, dta)
    a_cum_t = a_cum.T
    ea = jnp.exp(a_cum)
    to_end = jnp.exp(a_cum[q - 1:q, :] - a_cum) * dtv

    erow = lax.broadcasted_iota(jnp.int32, (M_DT_PAD, M_GW), 0)
    ehead = lax.broadcasted_iota(jnp.int32, (M_DT_PAD, M_GW), 1) // M_HEADDIM
    lane_head = lax.broadcasted_iota(jnp.int32, (q, M_GW), 1) // M_HEADDIM

    def conv(c0, width):
        acc = cb_ref[:, c0:c0 + width]
        for k in range(M_D_CONV):
            acc = acc + cw_ref[k:k + 1, c0:c0 + width] * xpad_ref[pl.ds(8 - (M_D_CONV - 1) + k, q), c0:c0 + width]
        return _silu(acc)

    for g in range(M_GROUPS):
        expand = jnp.where(erow == M_HPG * g + ehead, 1.0, 0.0).astype(BF16)
        dt_x = _dot_x_exact(dtv, expand)
        ea_x = _dot_x_exact(ea, expand)
        te_x = _dot_x_exact(to_end, expand)

        xg = conv(g * M_GW, M_GW)
        bg = conv(d_inner + g * M_D_STATE, M_D_STATE)
        cg = conv(d_inner + gn + g * M_D_STATE, M_D_STATE)
        cg_bf = cg.astype(BF16)
        cbm = _dot_nt(cg_bf, bg.astype(BF16))

        xdt = (xg * dt_x).astype(BF16)
        w_heads, x_heads = [], []
        for e in range(M_HPG):
            hd = M_HPG * g + e
            seg = a_cum[:, hd:hd + 1] - a_cum_t[hd:hd + 1, :]
            dec = jnp.where(causal, jnp.exp(jnp.minimum(seg, 0.0)), 0.0)
            w_heads.append((cbm * dec).astype(BF16))
            x_heads.append(jnp.where(lane_head == e, xdt, jnp.zeros_like(xdt)))
        y = _dot(jnp.concatenate(w_heads, axis=1), jnp.concatenate(x_heads, axis=0))

        st = st_ref[g]
        y = y + _dot(cg_bf, st.astype(BF16)) * ea_x
        xte = (xg * te_x).astype(BF16)
        st_ref[g] = st * ea_x[q - 1:q, :] + _dot(bg.T.astype(BF16), xte)

        y = y + xg * dskip_ref[:, g * M_GW:(g + 1) * M_GW]
        y = y * _silu(z_ref[:, g * M_GW:(g + 1) * M_GW])
        ms = jnp.mean(y * y, axis=-1, keepdims=True)
        yn = y * lax.rsqrt(ms + M_NORM_EPS) * nw_ref[:, g * M_GW:(g + 1) * M_GW]
        yn_ref[:, g * M_GW:(g + 1) * M_GW] = yn.astype(BF16)


def _mamba_ssd(z, xbc, dt, conv_w, conv_b, dt_bias, a_log, d_skip, norm_w, j, batch, seq):
    t, d_inner = z.shape
    conv_dim = xbc.shape[1]
    q = min(SSD_Q, seq)
    nq = seq // q
    tok = lambda b, c: (b * nq + c, 0)
    par = lambda b, c: (j, 0, 0)
    kern = functools.partial(_ssd_kernel, q=q, d_inner=d_inner)
    return pl.pallas_call(
        kern,
        out_shape=jax.ShapeDtypeStruct((t, d_inner), BF16),
        grid=(batch, nq),
        in_specs=[
            pl.BlockSpec((q, d_inner), tok),
            pl.BlockSpec((q, conv_dim), tok),
            pl.BlockSpec((q, M_DT_PAD), tok),
            pl.BlockSpec((None, M_D_CONV, conv_dim), par),
            pl.BlockSpec((None, 1, conv_dim), par),
            pl.BlockSpec((None, 1, M_DT_PAD), par),
            pl.BlockSpec((None, 1, M_DT_PAD), par),
            pl.BlockSpec((None, 1, d_inner), par),
            pl.BlockSpec((None, 1, d_inner), par),
        ],
        out_specs=pl.BlockSpec((q, d_inner), tok),
        scratch_shapes=[pltpu.VMEM((q + 8, conv_dim), F32),
                        pltpu.VMEM((M_GROUPS, M_D_STATE, M_GW), F32)],
        compiler_params=_params(("parallel", "arbitrary")),
        name="mamba_ssd",
    )(z, xbc, dt, conv_w, conv_b, dt_bias, a_log, d_skip, norm_w)


def _head_sum(x, bd_ref):
    d = x.shape[1]
    cols = [_dot_x_exact(x[:, c * R_BLK:(c + 1) * R_BLK], bd_ref[...]) for c in range(d // R_BLK)]
    return jnp.concatenate(cols, axis=1)


def _rwkv_proj_kernel(*refs, has_vres):
    (h_ref, hp_ref, g_ref, mix_ref, wrkv_ref, w1_ref, w2_ref, a1_ref, a2_ref, g1_ref, g2_ref,
     w0_ref, a0_ref, kk_ref, ka_ref, bd_ref) = refs[:16]
    if has_vres:
        v0_ref, v1_ref, v2_ref, vf_ref = refs[16:20]
        outs = refs[20:]
    else:
        outs = refs[16:]
    r_ref, lw_ref, k_ref, v_ref, a_ref, b_ref, gate_ref = outs

    gn = g_ref[...]
    u = _rmsnorm(h_ref[...], gn)
    u_before = _rmsnorm(hp_ref[7:8, :], gn)
    u_before = jnp.where(pl.program_id(1) == 0, jnp.zeros_like(u_before), u_before)
    row = lax.broadcasted_iota(jnp.int32, u.shape, 0)
    u_prev = jnp.where(row == 0, u_before, pltpu.roll(u, 1, 0))
    delta = u_prev - u

    def shift(i):
        return (u + delta * mix_ref[i:i + 1, :]).astype(BF16)

    r = _dot(shift(0), wrkv_ref[0])
    k = _dot(shift(2), wrkv_ref[1])
    xv = shift(3)
    v = _dot(xv, wrkv_ref[2])
    wl = w0_ref[...] + _dot(jnp.tanh(_dot(shift(1), w1_ref[...])).astype(BF16), w2_ref[...])
    w_log = -_softplus(-wl) - 0.5
    alpha = _sigmoid(a0_ref[...] + _dot(_dot(shift(4), a1_ref[...]).astype(BF16), a2_ref[...]))
    gate = _dot(_sigmoid(_dot(shift(5), g1_ref[...])).astype(BF16), g2_ref[...])
    if has_vres:
        mixv = _sigmoid(v0_ref[...] + _dot(_dot(xv, v1_ref[...]).astype(BF16), v2_ref[...]))
        v = v + (vf_ref[...] - v) * mixv

    kk = k * kk_ref[...]
    kk = kk / jnp.maximum(jnp.sqrt(_head_sum(kk * kk, bd_ref)), R_L2_EPS)

    r_ref[...] = r
    lw_ref[...] = -jnp.exp(w_log)
    k_ref[...] = k * (1.0 + (alpha - 1.0) * ka_ref[...])
    v_ref[...] = v
    a_ref[...] = -kk
    b_ref[...] = kk * alpha
    gate_ref[...] = gate


def _rwkv_proj(h, norms3, layer, j, p, v_first, batch, seq, tm):
    t, d = h.shape
    nt = seq // tm
    tok = lambda b, i: (b * nt + i, 0)
    prev = lambda b, i: (jnp.maximum((b * seq + i * tm) // 8 - 1, 0), 0)
    par3 = lambda b, i: (j, 0, 0)
    lp = R_LORA_PAD
    has_vres = v_first is not None
    in_specs = [
        pl.BlockSpec((tm, d), tok),
        pl.BlockSpec((8, d), prev),
        pl.BlockSpec((None, 1, d), lambda b, i: (layer * N_NORMS + 2, 0, 0)),
        pl.BlockSpec((None, 8, d), par3),
        _resident((None, 3, d, d), lambda b, i: (j, 0, 0, 0)),
        _resident((None, d, lp), par3), _resident((None, lp, d), par3),
        _resident((None, d, lp), par3), _resident((None, lp, d), par3),
        _resident((None, d, lp), par3), _resident((None, lp, d), par3),
        pl.BlockSpec((None, 1, d), par3), pl.BlockSpec((None, 1, d), par3),
        pl.BlockSpec((None, 1, d), par3), pl.BlockSpec((None, 1, d), par3),
        _resident((R_BLK, R_BLK), lambda b, i: (0, 0)),
    ]
    args = [h, h, norms3, p["mix"], p["w_rkv"], p["w1"], p["w2"], p["a1"], p["a2"], p["g1"], p["g2"],
            p["w0"], p["a0"], p["k_k"], p["k_a"], p["bd"]]
    if has_vres:
        parv = lambda b, i: (j - 1, 0, 0)
        in_specs += [pl.BlockSpec((None, 1, d), parv), _resident((None, d, lp), parv),
                     _resident((None, lp, d), parv), pl.BlockSpec((tm, d), tok)]
        args += [p["v0"], p["v1"], p["v2"], v_first]
    kern = functools.partial(_rwkv_proj_kernel, has_vres=has_vres)
    return pl.pallas_call(
        kern,
        out_shape=tuple(jax.ShapeDtypeStruct((t, d), F32) for _ in range(7)),
        grid=(batch, nt),
        in_specs=in_specs,
        out_specs=tuple(pl.BlockSpec((tm, d), tok) for _ in range(7)),
        compiler_params=_params(("parallel", "parallel")),
        name="rwkv_proj",
    )(*args)


def _rec_masks():
    n = 2 * R_CHUNK
    i = np.arange(n)[:, None]
    j = np.arange(n)[None, :]
    li, lj = i % R_CHUNK, j % R_CHUNK
    same = (i // R_CHUNK) == (j // R_CHUNK)
    masks = [same & (lj < li), same & (lj <= li), same & (lj < li) & (li // 8 == lj // 8)]
    for m in (8, 16, 32):
        masks.append(same & (li // (2 * m) == lj // (2 * m)) & (li % (2 * m) >= m) & (lj % (2 * m) < m))
    return jnp.asarray(np.stack(masks).astype(np.float32))


def _tri_inverse_offdiag(n_mat, masks_ref):
    d1 = n_mat * masks_ref[2]
    d1b = d1.astype(BF16)
    d2 = _dot(d1b, d1b)
    d2b = d2.astype(BF16)
    d4 = _dot(d2b, d2b)
    xa = d1 + d2 + _dot(d1b, d2b)
    x = xa + d4 + _dot(xa.astype(BF16), d4.astype(BF16))
    for lvl in range(3):
        nm = n_mat * masks_ref[3 + lvl]
        w = nm + _dot(nm.astype(BF16), x.astype(BF16))
        x = x + w + _dot(x.astype(BF16), w.astype(BF16))
    return x


def _rwkv_rec_kernel(r_ref, lw_ref, k_ref, v_ref, a_ref, b_ref, masks_ref, y_ref, st_ref, *, pairs, n_chunks):
    @pl.when(pl.program_id(2) == 0)
    def _():
        st_ref[...] = jnp.zeros_like(st_ref)

    lc = R_CHUNK
    ti = lax.broadcasted_iota(jnp.int32, (lc, lc), 0)
    tj = lax.broadcasted_iota(jnp.int32, (lc, lc), 1)
    tril = jnp.where(ti >= tj, 1.0, 0.0).astype(BF16)
    head0 = lax.broadcasted_iota(jnp.int32, (lc, R_PAIR), 1) < R_HEAD

    def stack(x):
        zero = jnp.zeros_like(x)
        return jnp.concatenate([jnp.where(head0, x, zero), jnp.where(head0, zero, x)], axis=0)

    def chunk(ci, carry):
        r0 = pl.multiple_of(ci * lc, lc)
        for p in range(pairs):
            sl = (pl.ds(r0, lc), slice(p * R_PAIR, (p + 1) * R_PAIR))
            lw = lw_ref[sl]
            logp = _dot_exact_x(tril, lw)
            logp_end = logp[lc - 1:lc, :]
            pinv = jnp.exp(-logp)
            to_end = jnp.exp(logp_end - logp)
            a_st = stack(a_ref[sl] * jnp.exp(logp - lw)).astype(BF16)
            q_st = stack(r_ref[sl] * jnp.exp(logp)).astype(BF16)
            kc = k_ref[sl]
            bc = b_ref[sl]
            lhs = jnp.concatenate([a_st, q_st], axis=0)
            rhs = jnp.concatenate([stack(bc * pinv), stack(kc * pinv)], axis=0).astype(BF16)
            a4 = _dot_nt(lhs, rhs)
            strict = masks_ref[0]
            incl = masks_ref[1]
            n_mat = a4[0:2 * lc, 0:2 * lc] * strict
            a_ak = (a4[0:2 * lc, 2 * lc:] * strict).astype(BF16)
            a_q = jnp.concatenate([a4[2 * lc:, 0:2 * lc] * incl, a4[2 * lc:, 2 * lc:] * incl], axis=1).astype(BF16)

            x = _tri_inverse_offdiag(n_mat, masks_ref)

            v_st = stack(v_ref[sl]).astype(BF16)
            st = st_ref[p]
            st_bf = st.astype(BF16)
            rhs_u = _dot_nt(a_st, st_bf) + _dot(a_ak, v_st)
            u_st = rhs_u + _dot(x.astype(BF16), rhs_u.astype(BF16))
            uv = jnp.concatenate([u_st.astype(BF16), v_st], axis=0)
            y_st = _dot_nt(q_st, st_bf) + _dot(a_q, uv)
            y_ref[sl] = y_st[0:lc] + y_st[lc:]
            bk = jnp.concatenate([stack(bc * to_end), stack(kc * to_end)], axis=0).astype(BF16)
            st_ref[p] = st * jnp.exp(logp_end) + _dot_tn(uv, bk)
        return carry

    lax.fori_loop(0, n_chunks, chunk, 0)


def _rwkv_recurrence(r, lw, k, v, a, b, batch, seq, lt, pairs):
    t, d = r.shape
    nt = seq // lt
    spec = pl.BlockSpec((lt, pairs * R_PAIR), lambda bi, pi, ti: (bi * nt + ti, pi))
    masks = _rec_masks()
    kern = functools.partial(_rwkv_rec_kernel, pairs=pairs, n_chunks=lt // R_CHUNK)
    return pl.pallas_call(
        kern,
        out_shape=jax.ShapeDtypeStruct((t, d), F32),
        grid=(batch, d // (pairs * R_PAIR), nt),
        in_specs=[spec] * 6 + [_resident(masks.shape, lambda bi, pi, ti: (0, 0, 0))],
        out_specs=spec,
        scratch_shapes=[pltpu.VMEM((pairs, R_PAIR, R_PAIR), F32)],
        compiler_params=_params(("parallel", "parallel", "arbitrary")),
        name="rwkv_recurrence",
    )(r, lw, k, v, a, b, masks)


def _rwkv_out_kernel(y_ref, r_ref, k_ref, v_ref, gate_ref, lw_ref, lb_ref, rk_ref, bd_ref, wo_ref,
                     h_ref, g_ref, o_ref):
    y = y_ref[...]
    inv_n = 1.0 / R_HEAD
    mu = _head_sum(y, bd_ref) * inv_n
    yc = y - mu
    var = _head_sum(yc * yc, bd_ref) * inv_n
    yn = yc * lax.rsqrt(var + R_LNX_EPS) * lw_ref[...] + lb_ref[...]
    v = v_ref[...]
    yn = yn + _head_sum(r_ref[...] * k_ref[...] * rk_ref[...], bd_ref) * v
    t = _dot((yn * gate_ref[...]).astype(BF16), wo_ref[...])
    o_ref[...] = h_ref[...] + _rmsnorm(t, g_ref[...])


def _rwkv_out(y, r, k, v, gate, p, j, h, norms3, layer, tm):
    t, d = h.shape
    tok = pl.BlockSpec((tm, d), lambda i: (i, 0))
    par = pl.BlockSpec((None, 1, d), lambda i: (j, 0, 0))
    return pl.pallas_call(
        _rwkv_out_kernel,
        out_shape=jax.ShapeDtypeStruct((t, d), F32),
        grid=(t // tm,),
        in_specs=[tok, tok, tok, tok, tok, par, par, par,
                  _resident((R_BLK, R_BLK), lambda i: (0, 0)),
                  _resident((None, d, d), lambda i: (j, 0, 0)),
                  tok,
                  pl.BlockSpec((None, 1, d), lambda i: (layer * N_NORMS + 3, 0, 0))],
        out_specs=tok,
        compiler_params=_params(("parallel",)),
        name="rwkv_out",
    )(y, r, k, v, gate, p["lnx_w"], p["lnx_b"], p["r_k"], p["bd"], p["w_o"], h, norms3)


def _pad_last(x, n):
    return jnp.pad(x, [(0, 0)] * (x.ndim - 1) + [(0, n - x.shape[-1])])


def _pad_axis(x, axis, n):
    pads = [(0, 0)] * x.ndim
    pads[axis] = (0, n - x.shape[axis])
    return jnp.pad(x, pads)


def _row(x):
    return x.reshape(x.shape[0], 1, -1)


def kernel(x, norms, ffn_w_in, ffn_w_out, m_in_proj, m_conv_w, m_conv_b, m_dt_bias, m_A_log, m_D, m_norm_w, m_out_proj, r_mix, r_w_rkv, r_w_o, r_w0, r_w1, r_w2, r_a0, r_a1, r_a2, r_v0, r_v1, r_v2, r_g1, r_g2, r_k_k, r_k_a, r_r_k, r_lnx_w, r_lnx_b):
    batch, seq, d = x.shape
    depth = norms.shape[0]
    t = batch * seq
    d_inner = m_out_proj.shape[1]
    conv_dim = m_conv_w.shape[2]
    m_heads = m_dt_bias.shape[1]
    assert m_heads <= M_DT_PAD and d_inner == M_GROUPS * M_GW and d % R_BLK == 0

    tm_ffn = min(512, t)
    tm_in = min(256, t)
    tm_out = min(512, t)
    tm_rp = min(256, seq)
    lt_rec = min(512, seq)

    norms3 = _row(norms.reshape(depth * N_NORMS, d))
    w_in = ffn_w_in.astype(BF16)
    w_out = ffn_w_out.astype(BF16)
    n_zx = d_inner + conv_dim
    m_w = jnp.concatenate([m_in_proj[:, :, :n_zx], _pad_last(m_in_proj[:, :, n_zx:], M_DT_PAD)], axis=2).astype(BF16)
    m_wo = m_out_proj.astype(BF16)
    m_cb = _row(m_conv_b)
    m_dtb = _row(_pad_last(m_dt_bias, M_DT_PAD))
    m_al = _row(_pad_last(m_A_log, M_DT_PAD))
    m_dsk = _row(jnp.repeat(m_D, M_HEADDIM, axis=1))
    m_nw = _row(m_norm_w)

    lp = R_LORA_PAD
    blk = np.arange(R_BLK) // R_HEAD
    rp = {
        "mix": _pad_axis(r_mix, 1, 8),
        "w_rkv": r_w_rkv.astype(BF16), "w_o": r_w_o.astype(BF16),
        "w1": _pad_last(r_w1, lp).astype(BF16), "w2": _pad_axis(r_w2, 1, lp).astype(BF16),
        "a1": _pad_last(r_a1, lp).astype(BF16), "a2": _pad_axis(r_a2, 1, lp).astype(BF16),
        "g1": _pad_last(r_g1, lp).astype(BF16), "g2": _pad_axis(r_g2, 1, lp).astype(BF16),
        "w0": _row(r_w0), "a0": _row(r_a0), "k_k": _row(r_k_k), "k_a": _row(r_k_a),
        "r_k": _row(r_r_k.reshape(r_r_k.shape[0], -1)), "lnx_w": _row(r_lnx_w), "lnx_b": _row(r_lnx_b),
        "bd": jnp.asarray((blk[:, None] == blk[None, :]).astype(np.float32), dtype=BF16),
    }
    if r_v0.shape[0] > 0:
        rp.update({"v0": _row(r_v0), "v1": _pad_last(r_v1, lp).astype(BF16),
                   "v2": _pad_axis(r_v2, 1, lp).astype(BF16)})

    h = x.reshape(t, d)
    v_first = None
    for i in range(depth):
        j = i // 2
        h = _ffn(h, norms3, w_in, w_out, i, 0, tm_ffn)
        if i % 2 == 0:
            z, xbc, dt = _mamba_in_proj(h, norms3, m_w, i, j, d_inner, conv_dim, tm_in)
            yn = _mamba_ssd(z, xbc, dt, m_conv_w, m_cb, m_dtb, m_al, m_dsk, m_nw, j, batch, seq)
            h = _out_proj(yn, m_wo, j, h, norms3, i, tm_out)
        else:
            r, lw, k, v, a, b, gate = _rwkv_proj(h, norms3, i, j, rp, v_first if j > 0 else None,
                                                 batch, seq, tm_rp)
            if j == 0:
                v_first = v
            y = _rwkv_recurrence(r, lw, k, v, a, b, batch, seq, lt_rec, 2)
            h = _rwkv_out(y, r, k, v, gate, rp, j, h, norms3, i, tm_out)
        h = _ffn(h, norms3, w_in, w_out, i, 1, tm_ffn)
    return h.reshape(batch, seq, d)
```

```python
import functools

import numpy as np
import jax
import jax.numpy as jnp
from jax import lax
from jax.experimental import pallas as pl
from jax.experimental.pallas import tpu as pltpu

F32 = jnp.float32
BF16 = jnp.bfloat16

NORM_EPS = 1e-6
N_NORMS = 6
FFN_HALF = 0.5

M_HEADDIM = 64
M_GROUPS = 8
M_HPG = 4
M_D_STATE = 128
M_D_CONV = 4
M_NORM_EPS = 1e-5
M_GW = M_HPG * M_HEADDIM
M_DT_PAD = 128
SSD_Q = 128

R_HEAD = 64
R_LNX_EPS = 64e-5
R_L2_EPS = 1e-12
R_LORA_PAD = 128
R_CHUNK = 64
R_PAIR = 2 * R_HEAD
R_BLK = 256

VMEM_LIMIT = 56 * 1024 * 1024


def _rmsnorm(x, g, eps=NORM_EPS):
    return x * lax.rsqrt(jnp.mean(x * x, axis=-1, keepdims=True) + eps) * g


def _sigmoid(x):
    return 1.0 / (1.0 + jnp.exp(-x))


def _silu(x):
    return x * _sigmoid(x)


def _softplus(x):
    return jnp.maximum(x, 0.0) + jnp.log(1.0 + jnp.exp(-jnp.abs(x)))


def _dot(a, b):
    return jnp.dot(a, b, preferred_element_type=F32)


def _dot_nt(a, b):
    return lax.dot_general(a, b, (((1,), (1,)), ((), ())), preferred_element_type=F32)


def _dot_tn(a, b):
    return lax.dot_general(a, b, (((0,), (0,)), ((), ())), preferred_element_type=F32)


def _split(x):
    hi = x.astype(BF16)
    lo = (x - hi.astype(F32)).astype(BF16)
    return hi, lo


def _dot_x_exact(x, m):
    hi, lo = _split(x)
    return _dot(hi, m) + _dot(lo, m)


def _dot_exact_x(m, x):
    hi, lo = _split(x)
    return _dot(m, hi) + _dot(m, lo)


def _params(sem):
    return pltpu.CompilerParams(dimension_semantics=sem, vmem_limit_bytes=VMEM_LIMIT)


def _resident(block, index_map):
    return pl.BlockSpec(block, index_map, pipeline_mode=pl.Buffered(1))


def _ffn_kernel(x_ref, gpre_ref, gpost_ref, win_ref, wout_ref, o_ref, acc_ref, *, d_ff, cf):
    x = x_ref[...]
    xn = _rmsnorm(x, gpre_ref[...]).astype(BF16)
    for c in range(d_ff // cf):
        gate = _dot(xn, win_ref[:, c * cf:(c + 1) * cf])
        up = _dot(xn, win_ref[:, d_ff + c * cf:d_ff + (c + 1) * cf])
        act = (_silu(gate) * up).astype(BF16)
        part = _dot(act, wout_ref[c * cf:(c + 1) * cf, :])
        if c == 0:
            acc_ref[...] = part
        else:
            acc_ref[...] += part
    o_ref[...] = x + FFN_HALF * _rmsnorm(acc_ref[...], gpost_ref[...])


def _ffn(h, norms3, w_in, w_out, layer, which, tm):
    t, d = h.shape
    d_ff = w_out.shape[2]
    n0 = layer * N_NORMS + (0 if which == 0 else 4)
    kern = functools.partial(_ffn_kernel, d_ff=d_ff, cf=256)
    return pl.pallas_call(
        kern,
        out_shape=jax.ShapeDtypeStruct((t, d), F32),
        grid=(t // tm,),
        in_specs=[
            pl.BlockSpec((tm, d), lambda i: (i, 0)),
            pl.BlockSpec((None, 1, d), lambda i: (n0, 0, 0)),
            pl.BlockSpec((None, 1, d), lambda i: (n0 + 1, 0, 0)),
            _resident((None, None, d, 2 * d_ff), lambda i: (layer, which, 0, 0)),
            _resident((None, None, d_ff, d), lambda i: (layer, which, 0, 0)),
        ],
        out_specs=pl.BlockSpec((tm, d), lambda i: (i, 0)),
        scratch_shapes=[pltpu.VMEM((tm, d), F32)],
        compiler_params=_params(("parallel",)),
        name="ffn",
    )(h, norms3, norms3, w_in, w_out)


def _in_proj_kernel(h_ref, g_ref, w_ref, z_ref, xbc_ref, dt_ref, *, d_inner, conv_dim, cn):
    u = _rmsnorm(h_ref[...], g_ref[...]).astype(BF16)
    for c in range(d_inner // cn):
        z_ref[:, c * cn:(c + 1) * cn] = _dot(u, w_ref[:, c * cn:(c + 1) * cn])
    for c in range(conv_dim // cn):
        xbc_ref[:, c * cn:(c + 1) * cn] = _dot(u, w_ref[:, d_inner + c * cn:d_inner + (c + 1) * cn])
    dt_ref[...] = _dot(u, w_ref[:, d_inner + conv_dim:])


def _mamba_in_proj(h, norms3, w, layer, j, d_inner, conv_dim, tm):
    t, d = h.shape
    n_all = w.shape[2]
    kern = functools.partial(_in_proj_kernel, d_inner=d_inner, conv_dim=conv_dim, cn=512)
    return pl.pallas_call(
        kern,
        out_shape=(jax.ShapeDtypeStruct((t, d_inner), F32),
                   jax.ShapeDtypeStruct((t, conv_dim), F32),
                   jax.ShapeDtypeStruct((t, M_DT_PAD), F32)),
        grid=(t // tm,),
        in_specs=[
            pl.BlockSpec((tm, d), lambda i: (i, 0)),
            pl.BlockSpec((None, 1, d), lambda i: (layer * N_NORMS + 2, 0, 0)),
            _resident((None, d, n_all), lambda i: (j, 0, 0)),
        ],
        out_specs=(pl.BlockSpec((tm, d_inner), lambda i: (i, 0)),
                   pl.BlockSpec((tm, conv_dim), lambda i: (i, 0)),
                   pl.BlockSpec((tm, M_DT_PAD), lambda i: (i, 0))),
        compiler_params=_params(("parallel",)),
        name="mamba_in_proj",
    )(h, norms3, w)


def _out_proj_kernel(y_ref, w_ref, h_ref, g_ref, o_ref):
    t = _dot(y_ref[...], w_ref[...])
    o_ref[...] = h_ref[...] + _rmsnorm(t, g_ref[...])


def _out_proj(y, w, j, h, norms3, layer, tm):
    t, d = h.shape
    k = y.shape[1]
    return pl.pallas_call(
        _out_proj_kernel,
        out_shape=jax.ShapeDtypeStruct((t, d), F32),
        grid=(t // tm,),
        in_specs=[
            pl.BlockSpec((tm, k), lambda i: (i, 0)),
            _resident((None, k, d), lambda i: (j, 0, 0)),
            pl.BlockSpec((tm, d), lambda i: (i, 0)),
            pl.BlockSpec((None, 1, d), lambda i: (layer * N_NORMS + 3, 0, 0)),
        ],
        out_specs=pl.BlockSpec((tm, d), lambda i: (i, 0)),
        compiler_params=_params(("parallel",)),
        name="out_proj",
    )(y, w, h, norms3)


def _ssd_kernel(z_ref, xbc_ref, dt_ref, cw_ref, cb_ref, dtb_ref, alog_ref, dskip_ref, nw_ref,
                yn_ref, xpad_ref, st_ref, *, q, d_inner):
    c = pl.program_id(1)
    gn = M_GROUPS * M_D_STATE

    @pl.when(c == 0)
    def _():
        xpad_ref[0:8, :] = jnp.zeros((8, xpad_ref.shape[1]), F32)
        st_ref[...] = jnp.zeros_like(st_ref)

    @pl.when(c > 0)
    def _():
        xpad_ref[0:8, :] = xpad_ref[q:q + 8, :]

    xpad_ref[8:q + 8, :] = xbc_ref[...]

    row = lax.broadcasted_iota(jnp.int32, (q, q), 0)
    col = lax.broadcasted_iota(jnp.int32, (q, q), 1)
    causal = row >= col
    tril = jnp.where(causal, 1.0, 0.0).astype(BF16)

    dtv = _softplus(dt_ref[...] + dtb_ref[...])
    dta = dtv * (-jnp.exp(alog_ref[...]))
    a_cum = _dot_exact_x(tril, dta)
    a_cum_t = a_cum.T
    ea = jnp.exp(a_cum)
    to_end = jnp.exp(a_cum[q - 1:q, :] - a_cum) * dtv

    erow = lax.broadcasted_iota(jnp.int32, (M_DT_PAD, M_GW), 0)
    ehead = lax.broadcasted_iota(jnp.int32, (M_DT_PAD, M_GW), 1) // M_HEADDIM
    lane_head = lax.broadcasted_iota(jnp.int32, (q, M_GW), 1) // M_HEADDIM

    def conv(c0, width):
        acc = cb_ref[:, c0:c0 + width]
        for k in range(M_D_CONV):
            acc = acc + cw_ref[k:k + 1, c0:c0 + width] * xpad_ref[pl.ds(8 - (M_D_CONV - 1) + k, q), c0:c0 + width]
        return _silu(acc)

    for g in range(M_GROUPS):
        expand = jnp.where(erow == M_HPG * g + ehead, 1.0, 0.0).astype(BF16)
        dt_x = _dot_x_exact(dtv, expand)
        ea_x = _dot_x_exact(ea, expand)
        te_x = _dot_x_exact(to_end, expand)

        xg = conv(g * M_GW, M_GW)
        bg = conv(d_inner + g * M_D_STATE, M_D_STATE)
        cg = conv(d_inner + gn + g * M_D_STATE, M_D_STATE)
        cg_bf = cg.astype(BF16)
        cbm = _dot_nt(cg_bf, bg.astype(BF16))

        xdt = (xg * dt_x).astype(BF16)
        w_heads, x_heads = [], []
        for e in range(M_HPG):
            hd = M_HPG * g + e
            seg = a_cum[:, hd:hd + 1] - a_cum_t[hd:hd + 1, :]
            dec = jnp.where(causal, jnp.exp(jnp.minimum(seg, 0.0)), 0.0)
            w_heads.append((cbm * dec).astype(BF16))
            x_heads.append(jnp.where(lane_head == e, xdt, jnp.zeros_like(xdt)))
        y = _dot(jnp.concatenate(w_heads, axis=1), jnp.concatenate(x_heads, axis=0))

        st = st_ref[g]
        y = y + _dot(cg_bf, st.astype(BF16)) * ea_x
        xte = (xg * te_x).astype(BF16)
        st_ref[g] = st * ea_x[q - 1:q, :] + _dot(bg.T.astype(BF16), xte)

        y = y + xg * dskip_ref[:, g * M_GW:(g + 1) * M_GW]
        y = y * _silu(z_ref[:, g * M_GW:(g + 1) * M_GW])
        ms = jnp.mean(y * y, axis=-1, keepdims=True)
        yn = y * lax.rsqrt(ms + M_NORM_EPS) * nw_ref[:, g * M_GW:(g + 1) * M_GW]
        yn_ref[:, g * M_GW:(g + 1) * M_GW] = yn.astype(BF16)


def _mamba_ssd(z, xbc, dt, conv_w, conv_b, dt_bias, a_log, d_skip, norm_w, j, batch, seq):
    t, d_inner = z.shape
    conv_dim = xbc.shape[1]
    q = min(SSD_Q, seq)
    nq = seq // q
    tok = lambda b, c: (b * nq + c, 0)
    par = lambda b, c: (j, 0, 0)
    kern = functools.partial(_ssd_kernel, q=q, d_inner=d_inner)
    return pl.pallas_call(
        kern,
        out_shape=jax.ShapeDtypeStruct((t, d_inner), BF16),
        grid=(batch, nq),
        in_specs=[
            pl.BlockSpec((q, d_inner), tok),
            pl.BlockSpec((q, conv_dim), tok),
            pl.BlockSpec((q, M_DT_PAD), tok),
            pl.BlockSpec((None, M_D_CONV, conv_dim), par),
            pl.BlockSpec((None, 1, conv_dim), par),
            pl.BlockSpec((None, 1, M_DT_PAD), par),
            pl.BlockSpec((None, 1, M_DT_PAD), par),
            pl.BlockSpec((None, 1, d_inner), par),
            pl.BlockSpec((None, 1, d_inner), par),
        ],
        out_specs=pl.BlockSpec((q, d_inner), tok),
        scratch_shapes=[pltpu.VMEM((q + 8, conv_dim), F32),
                        pltpu.VMEM((M_GROUPS, M_D_STATE, M_GW), F32)],
        compiler_params=_params(("parallel", "arbitrary")),
        name="mamba_ssd",
    )(z, xbc, dt, conv_w, conv_b, dt_bias, a_log, d_skip, norm_w)


def _head_sum(x, bd_ref):
    d = x.shape[1]
    cols = [_dot_x_exact(x[:, c * R_BLK:(c + 1) * R_BLK], bd_ref[...]) for c in range(d // R_BLK)]
    return jnp.concatenate(cols, axis=1)


def _rwkv_proj_kernel(*refs, has_vres):
    (h_ref, hp_ref, g_ref, mix_ref, wrkv_ref, w1_ref, w2_ref, a1_ref, a2_ref, g1_ref, g2_ref,
     w0_ref, a0_ref, kk_ref, ka_ref, bd_ref) = refs[:16]
    if has_vres:
        v0_ref, v1_ref, v2_ref, vf_ref = refs[16:20]
        outs = refs[20:]
    else:
        outs = refs[16:]
    r_ref, lw_ref, k_ref, v_ref, a_ref, b_ref, gate_ref = outs

    gn = g_ref[...]
    u = _rmsnorm(h_ref[...], gn)
    u_before = _rmsnorm(hp_ref[7:8, :], gn)
    u_before = jnp.where(pl.program_id(1) == 0, jnp.zeros_like(u_before), u_before)
    row = lax.broadcasted_iota(jnp.int32, u.shape, 0)
    u_prev = jnp.where(row == 0, u_before, pltpu.roll(u, 1, 0))
    delta = u_prev - u

    def shift(i):
        return (u + delta * mix_ref[i:i + 1, :]).astype(BF16)

    r = _dot(shift(0), wrkv_ref[0])
    k = _dot(shift(2), wrkv_ref[1])
    xv = shift(3)
    v = _dot(xv, wrkv_ref[2])
    wl = w0_ref[...] + _dot(jnp.tanh(_dot(shift(1), w1_ref[...])).astype(BF16), w2_ref[...])
    w_log = -_softplus(-wl) - 0.5
    alpha = _sigmoid(a0_ref[...] + _dot(_dot(shift(4), a1_ref[...]).astype(BF16), a2_ref[...]))
    gate = _dot(_sigmoid(_dot(shift(5), g1_ref[...])).astype(BF16), g2_ref[...])
    if has_vres:
        mixv = _sigmoid(v0_ref[...] + _dot(_dot(xv, v1_ref[...]).astype(BF16), v2_ref[...]))
        v = v + (vf_ref[...] - v) * mixv

    kk = k * kk_ref[...]
    kk = kk / jnp.maximum(jnp.sqrt(_head_sum(kk * kk, bd_ref)), R_L2_EPS)

    r_ref[...] = r
    lw_ref[...] = -jnp.exp(w_log)
    k_ref[...] = k * (1.0 + (alpha - 1.0) * ka_ref[...])
    v_ref[...] = v
    a_ref[...] = -kk
    b_ref[...] = kk * alpha
    gate_ref[...] = gate


def _rwkv_proj(h, norms3, layer, j, p, v_first, batch, seq, tm):
    t, d = h.shape
    nt = seq // tm
    tok = lambda b, i: (b * nt + i, 0)
    prev = lambda b, i: (jnp.maximum((b * seq + i * tm) // 8 - 1, 0), 0)
    par3 = lambda b, i: (j, 0, 0)
    lp = R_LORA_PAD
    has_vres = v_first is not None
    in_specs = [
        pl.BlockSpec((tm, d), tok),
        pl.BlockSpec((8, d), prev),
        pl.BlockSpec((None, 1, d), lambda b, i: (layer * N_NORMS + 2, 0, 0)),
        pl.BlockSpec((None, 8, d), par3),
        _resident((None, 3, d, d), lambda b, i: (j, 0, 0, 0)),
        _resident((None, d, lp), par3), _resident((None, lp, d), par3),
        _resident((None, d, lp), par3), _resident((None, lp, d), par3),
        _resident((None, d, lp), par3), _resident((None, lp, d), par3),
        pl.BlockSpec((None, 1, d), par3), pl.BlockSpec((None, 1, d), par3),
        pl.BlockSpec((None, 1, d), par3), pl.BlockSpec((None, 1, d), par3),
        _resident((R_BLK, R_BLK), lambda b, i: (0, 0)),
    ]
    args = [h, h, norms3, p["mix"], p["w_rkv"], p["w1"], p["w2"], p["a1"], p["a2"], p["g1"], p["g2"],
            p["w0"], p["a0"], p["k_k"], p["k_a"], p["bd"]]
    if has_vres:
        parv = lambda b, i: (j - 1, 0, 0)
        in_specs += [pl.BlockSpec((None, 1, d), parv), _resident((None, d, lp), parv),
                     _resident((None, lp, d), parv), pl.BlockSpec((tm, d), tok)]
        args += [p["v0"], p["v1"], p["v2"], v_first]
    kern = functools.partial(_rwkv_proj_kernel, has_vres=has_vres)
    return pl.pallas_call(
        kern,
        out_shape=tuple(jax.ShapeDtypeStruct((t, d), F32) for _ in range(7)),
        grid=(batch, nt),
        in_specs=in_specs,
        out_specs=tuple(pl.BlockSpec((tm, d), tok) for _ in range(7)),
        compiler_params=_params(("parallel", "parallel")),
        name="rwkv_proj",
    )(*args)


def _rec_masks():
    n = 2 * R_CHUNK
    i = np.arange(n)[:, None]
    j = np.arange(n)[None, :]
    li, lj = i % R_CHUNK, j % R_CHUNK
    same = (i // R_CHUNK) == (j // R_CHUNK)
    masks = [same & (lj < li), same & (lj <= li), same & (lj < li) & (li // 8 == lj // 8)]
    for m in (8, 16, 32):
        masks.append(same & (li // (2 * m) == lj // (2 * m)) & (li % (2 * m) >= m) & (lj % (2 * m) < m))
    return jnp.asarray(np.stack(masks).astype(np.float32))


def _tri_inverse_offdiag(n_mats, masks_ref):
    bf = lambda xs: [x.astype(BF16) for x in xs]
    d1 = [n * masks_ref[2] for n in n_mats]
    d1b = bf(d1)
    d2 = [_dot(a, a) for a in d1b]
    d2b = bf(d2)
    d4 = [_dot(a, a) for a in d2b]
    d3 = [_dot(a, b) for a, b in zip(d1b, d2b)]
    xa = [a + b + c for a, b, c in zip(d1, d2, d3)]
    xd = [_dot(a, b) for a, b in zip(bf(xa), bf(d4))]
    x = [a + b + c for a, b, c in zip(xa, d4, xd)]
    for lvl in range(3):
        nm = [n * masks_ref[3 + lvl] for n in n_mats]
        nx = [_dot(a, b) for a, b in zip(bf(nm), bf(x))]
        w = [a + b for a, b in zip(nm, nx)]
        xw = [_dot(a, b) for a, b in zip(bf(x), bf(w))]
        x = [a + b + c for a, b, c in zip(x, w, xw)]
    return x


def _rwkv_rec_kernel(r_ref, lw_ref, k_ref, v_ref, a_ref, b_ref, masks_ref, y_ref,
                     st_ref, qp_ref, g_ref, ct_ref, pe_ref, *, pairs, n_chunks, unroll):
    @pl.when(pl.program_id(2) == 0)
    def _():
        st_ref[...] = jnp.zeros_like(st_ref)

    lc = R_CHUNK
    ti = lax.broadcasted_iota(jnp.int32, (lc, lc), 0)
    tj = lax.broadcasted_iota(jnp.int32, (lc, lc), 1)
    tril = jnp.where(ti >= tj, 1.0, 0.0).astype(BF16)
    head0 = lax.broadcasted_iota(jnp.int32, (lc, R_PAIR), 1) < R_HEAD

    def stack(x):
        zero = jnp.zeros_like(x)
        return jnp.concatenate([jnp.where(head0, x, zero), jnp.where(head0, zero, x)], axis=0)

    def unstack(x_st):
        return x_st[0:lc] + x_st[lc:]

    def build(insts):
        n = len(insts)
        each = lambda f, *ls: [f(*xs) for xs in zip(*ls)]
        sls = [(pl.ds(pl.multiple_of(ci * lc, lc), lc), slice(p * R_PAIR, (p + 1) * R_PAIR)) for ci, p in insts]
        lw = [lw_ref[sl] for sl in sls]
        logp = [_dot_exact_x(tril, x) for x in lw]
        logp_end = [x[lc - 1:lc, :] for x in logp]
        pinv = [jnp.exp(-x) for x in logp]
        to_end = each(lambda e, x: jnp.exp(e - x), logp_end, logp)
        a_st = [stack(a_ref[sl] * jnp.exp(x - w)) for sl, x, w in zip(sls, logp, lw)]
        q_un = [r_ref[sl] * jnp.exp(x) for sl, x in zip(sls, logp)]
        kc = [k_ref[sl] for sl in sls]
        bc = [b_ref[sl] for sl in sls]
        lhs = each(lambda a, q: jnp.concatenate([a, stack(q)], axis=0).astype(BF16), a_st, q_un)
        rhs = each(lambda b, k, pi: jnp.concatenate([stack(b * pi), stack(k * pi)], axis=0).astype(BF16),
                   bc, kc, pinv)
        a4 = each(_dot_nt, lhs, rhs)
        strict = masks_ref[0]
        incl = masks_ref[1]
        n_mat = [x[0:2 * lc, 0:2 * lc] * strict for x in a4]
        a_ak = [(x[0:2 * lc, 2 * lc:] * strict).astype(BF16) for x in a4]
        aqb_un = [unstack(x[2 * lc:, 0:2 * lc] * incl).astype(BF16) for x in a4]
        aqk_un = [unstack(x[2 * lc:, 2 * lc:] * incl).astype(BF16) for x in a4]

        v_st = [stack(v_ref[sl]).astype(BF16) for sl in sls]
        z = each(_dot, a_ak, v_st)
        x = [m.astype(BF16) for m in _tri_inverse_offdiag(n_mat, masks_ref)]
        az = each(lambda a, zz: jnp.concatenate([a, zz], axis=1), a_st, z)
        xaz = each(lambda xx, m: _dot(xx, m.astype(BF16)), x, az)
        au_bf = each(lambda m, d: (m + d).astype(BF16), az, xaz)
        qy = each(_dot, aqb_un, au_bf)
        y0 = each(_dot, aqk_un, v_st)
        bt = each(lambda b, e: stack(b * e).astype(BF16), bc, to_end)
        kt = each(lambda k, e: stack(k * e).astype(BF16), kc, to_end)
        g = each(lambda m, b: _dot_tn(m[:, 0:R_PAIR], b), au_bf, bt)
        ct = each(lambda m, v, b, k: _dot_tn(jnp.concatenate([m[:, R_PAIR:], v], axis=0),
                                             jnp.concatenate([b, k], axis=0)), au_bf, v_st, bt, kt)
        for i in range(n):
            ci, p = insts[i]
            qp_ref[sls[i]] = (q_un[i] + qy[i][:, 0:R_PAIR]).astype(BF16)
            y_ref[sls[i]] = qy[i][:, R_PAIR:] + y0[i]
            g_ref[ci, p] = g[i].astype(BF16)
            ct_ref[ci, p] = ct[i]
            pe_ref[ci, p] = jnp.broadcast_to(jnp.exp(logp_end[i]), (8, R_PAIR))

    def phase_a(i, carry):
        build([(i * unroll + u, p) for u in range(unroll) for p in range(pairs)])
        return carry

    lax.fori_loop(0, n_chunks // unroll, phase_a, 0)

    def phase_b(ci, carry):
        sls = [(pl.ds(pl.multiple_of(ci * lc, lc), lc), slice(p * R_PAIR, (p + 1) * R_PAIR))
               for p in range(pairs)]
        st = [st_ref[p] for p in range(pairs)]
        st_bf = [s.astype(BF16) for s in st]
        sg = [_dot(st_bf[p], g_ref[ci, p]) for p in range(pairs)]
        yq = [_dot_nt(qp_ref[sls[p]], st_bf[p]) for p in range(pairs)]
        for p in range(pairs):
            st_ref[p] = st[p] * pe_ref[ci, p][0:1, :] + sg[p] + ct_ref[ci, p]
            y_ref[sls[p]] += yq[p]
        return carry

    lax.fori_loop(0, n_chunks, phase_b, 0)


def _rwkv_recurrence(r, lw, k, v, a, b, batch, seq, lt, pairs):
    t, d = r.shape
    nt = seq // lt
    nch = lt // R_CHUNK
    spec = pl.BlockSpec((lt, pairs * R_PAIR), lambda bi, pi, ti: (bi * nt + ti, pi))
    masks = _rec_masks()
    kern = functools.partial(_rwkv_rec_kernel, pairs=pairs, n_chunks=nch, unroll=4 if nch % 4 == 0 else 1)
    return pl.pallas_call(
        kern,
        out_shape=jax.ShapeDtypeStruct((t, d), F32),
        grid=(batch, d // (pairs * R_PAIR), nt),
        in_specs=[spec] * 6 + [_resident(masks.shape, lambda bi, pi, ti: (0, 0, 0))],
        out_specs=spec,
        scratch_shapes=[pltpu.VMEM((pairs, R_PAIR, R_PAIR), F32),
                        pltpu.VMEM((lt, pairs * R_PAIR), BF16),
                        pltpu.VMEM((nch, pairs, R_PAIR, R_PAIR), BF16),
                        pltpu.VMEM((nch, pairs, R_PAIR, R_PAIR), F32),
                        pltpu.VMEM((nch, pairs, 8, R_PAIR), F32)],
        compiler_params=_params(("parallel", "parallel", "arbitrary")),
        name="rwkv_recurrence",
    )(r, lw, k, v, a, b, masks)


def _rwkv_out_kernel(y_ref, r_ref, k_ref, v_ref, gate_ref, lw_ref, lb_ref, rk_ref, bd_ref, wo_ref,
                     h_ref, g_ref, o_ref):
    y = y_ref[...]
    inv_n = 1.0 / R_HEAD
    mu = _head_sum(y, bd_ref) * inv_n
    yc = y - mu
    var = _head_sum(yc * yc, bd_ref) * inv_n
    yn = yc * lax.rsqrt(var + R_LNX_EPS) * lw_ref[...] + lb_ref[...]
    v = v_ref[...]
    yn = yn + _head_sum(r_ref[...] * k_ref[...] * rk_ref[...], bd_ref) * v
    t = _dot((yn * gate_ref[...]).astype(BF16), wo_ref[...])
    o_ref[...] = h_ref[...] + _rmsnorm(t, g_ref[...])


def _rwkv_out(y, r, k, v, gate, p, j, h, norms3, layer, tm):
    t, d = h.shape
    tok = pl.BlockSpec((tm, d), lambda i: (i, 0))
    par = pl.BlockSpec((None, 1, d), lambda i: (j, 0, 0))
    return pl.pallas_call(
        _rwkv_out_kernel,
        out_shape=jax.ShapeDtypeStruct((t, d), F32),
        grid=(t // tm,),
        in_specs=[tok, tok, tok, tok, tok, par, par, par,
                  _resident((R_BLK, R_BLK), lambda i: (0, 0)),
                  _resident((None, d, d), lambda i: (j, 0, 0)),
                  tok,
                  pl.BlockSpec((None, 1, d), lambda i: (layer * N_NORMS + 3, 0, 0))],
        out_specs=tok,
        compiler_params=_params(("parallel",)),
        name="rwkv_out",
    )(y, r, k, v, gate, p["lnx_w"], p["lnx_b"], p["r_k"], p["bd"], p["w_o"], h, norms3)


def _pad_last(x, n):
    return jnp.pad(x, [(0, 0)] * (x.ndim - 1) + [(0, n - x.shape[-1])])


def _pad_axis(x, axis, n):
    pads = [(0, 0)] * x.ndim
    pads[axis] = (0, n - x.shape[axis])
    return jnp.pad(x, pads)


def _row(x):
    return x.reshape(x.shape[0], 1, -1)


def kernel(x, norms, ffn_w_in, ffn_w_out, m_in_proj, m_conv_w, m_conv_b, m_dt_bias, m_A_log, m_D, m_norm_w, m_out_proj, r_mix, r_w_rkv, r_w_o, r_w0, r_w1, r_w2, r_a0, r_a1, r_a2, r_v0, r_v1, r_v2, r_g1, r_g2, r_k_k, r_k_a, r_r_k, r_lnx_w, r_lnx_b):
    batch, seq, d = x.shape
    depth = norms.shape[0]
    t = batch * seq
    d_inner = m_out_proj.shape[1]
    conv_dim = m_conv_w.shape[2]
    m_heads = m_dt_bias.shape[1]
    assert m_heads <= M_DT_PAD and d_inner == M_GROUPS * M_GW and d % R_BLK == 0

    tm_ffn = min(512, t)
    tm_in = min(256, t)
    tm_out = min(512, t)
    tm_rp = min(256, seq)
    lt_rec = min(512, seq)

    norms3 = _row(norms.reshape(depth * N_NORMS, d))
    w_in = ffn_w_in.astype(BF16)
    w_out = ffn_w_out.astype(BF16)
    n_zx = d_inner + conv_dim
    m_w = jnp.concatenate([m_in_proj[:, :, :n_zx], _pad_last(m_in_proj[:, :, n_zx:], M_DT_PAD)], axis=2).astype(BF16)
    m_wo = m_out_proj.astype(BF16)
    m_cb = _row(m_conv_b)
    m_dtb = _row(_pad_last(m_dt_bias, M_DT_PAD))
    m_al = _row(_pad_last(m_A_log, M_DT_PAD))
    m_dsk = _row(jnp.repeat(m_D, M_HEADDIM, axis=1))
    m_nw = _row(m_norm_w)

    lp = R_LORA_PAD
    blk = np.arange(R_BLK) // R_HEAD
    rp = {
        "mix": _pad_axis(r_mix, 1, 8),
        "w_rkv": r_w_rkv.astype(BF16), "w_o": r_w_o.astype(BF16),
        "w1": _pad_last(r_w1, lp).astype(BF16), "w2": _pad_axis(r_w2, 1, lp).astype(BF16),
        "a1": _pad_last(r_a1, lp).astype(BF16), "a2": _pad_axis(r_a2, 1, lp).astype(BF16),
        "g1": _pad_last(r_g1, lp).astype(BF16), "g2": _pad_axis(r_g2, 1, lp).astype(BF16),
        "w0": _row(r_w0), "a0": _row(r_a0), "k_k": _row(r_k_k), "k_a": _row(r_k_a),
        "r_k": _row(r_r_k.reshape(r_r_k.shape[0], -1)), "lnx_w": _row(r_lnx_w), "lnx_b": _row(r_lnx_b),
        "bd": jnp.asarray((blk[:, None] == blk[None, :]).astype(np.float32), dtype=BF16),
    }
    if r_v0.shape[0] > 0:
        rp.update({"v0": _row(r_v0), "v1": _pad_last(r_v1, lp).astype(BF16),
                   "v2": _pad_axis(r_v2, 1, lp).astype(BF16)})

    h = x.reshape(t, d)
    v_first = None
    for i in range(depth):
        j = i // 2
        h = _ffn(h, norms3, w_in, w_out, i, 0, tm_ffn)
        if i % 2 == 0:
            z, xbc, dt = _mamba_in_proj(h, norms3, m_w, i, j, d_inner, conv_dim, tm_in)
            yn = _mamba_ssd(z, xbc, dt, m_conv_w, m_cb, m_dtb, m_al, m_dsk, m_nw, j, batch, seq)
            h = _out_proj(yn, m_wo, j, h, norms3, i, tm_out)
        else:
            r, lw, k, v, a, b, gate = _rwkv_proj(h, norms3, i, j, rp, v_first if j > 0 else None,
                                                 batch, seq, tm_rp)
            if j == 0:
                v_first = v
            y = _rwkv_recurrence(r, lw, k, v, a, b, batch, seq, lt_rec, 2)
            h = _rwkv_out(y, r, k, v, gate, rp, j, h, norms3, i, tm_out)
        h = _ffn(h, norms3, w_in, w_out, i, 1, tm_ffn)
    return h.reshape(batch, seq, d)
```

```python
import functools

import numpy as np
import jax
import jax.numpy as jnp
from jax import lax
from jax.experimental import pallas as pl
from jax.experimental.pallas import tpu as pltpu

F32 = jnp.float32
BF16 = jnp.bfloat16

NORM_EPS = 1e-6
N_NORMS = 6
FFN_HALF = 0.5

M_HEADDIM = 64
M_GROUPS = 8
M_HPG = 4
M_D_STATE = 128
M_D_CONV = 4
M_NORM_EPS = 1e-5
M_GW = M_HPG * M_HEADDIM
M_DT_PAD = 128
SSD_Q = 128

R_HEAD = 64
R_LNX_EPS = 64e-5
R_L2_EPS = 1e-12
R_LORA_PAD = 128
R_CHUNK = 64
R_QUAD = 4 * R_HEAD

VMEM_LIMIT = 56 * 1024 * 1024


def _rmsnorm(x, g, eps=NORM_EPS):
    return x * lax.rsqrt(jnp.mean(x * x, axis=-1, keepdims=True) + eps) * g


def _sigmoid(x):
    return 0.5 + 0.5 * jnp.tanh(0.5 * x)


def _silu(x):
    half = 0.5 * x
    return half + half * jnp.tanh(half)


def _softplus(x):
    return jnp.maximum(x, 0.0) + jnp.log(1.0 + jnp.exp(-jnp.abs(x)))


def _dot(a, b):
    return jnp.dot(a, b, preferred_element_type=F32)


def _dot_nt(a, b):
    return lax.dot_general(a, b, (((1,), (1,)), ((), ())), preferred_element_type=F32)


def _dot_tn(a, b):
    return lax.dot_general(a, b, (((0,), (0,)), ((), ())), preferred_element_type=F32)


def _split(x):
    hi = x.astype(BF16)
    lo = (x - hi.astype(F32)).astype(BF16)
    return hi, lo


def _dot_split(hl, m):
    return _dot(hl[0], m) + _dot(hl[1], m)


def _dot_x_exact(x, m):
    return _dot_split(_split(x), m)


def _dot_exact_x(m, x):
    hi, lo = _split(x)
    return _dot(m, hi) + _dot(m, lo)


def _params(sem):
    return pltpu.CompilerParams(dimension_semantics=sem, vmem_limit_bytes=VMEM_LIMIT)


def _resident(block, index_map):
    return pl.BlockSpec(block, index_map, pipeline_mode=pl.Buffered(1))


def _ffn_kernel(x_ref, gpre_ref, gpost_ref, win_ref, wout_ref, o_ref, acc_ref, *, d_ff, cf):
    x = x_ref[...]
    xn = _rmsnorm(x, gpre_ref[...]).astype(BF16)
    for c in range(d_ff // cf):
        gate = _dot(xn, win_ref[:, c * cf:(c + 1) * cf])
        up = _dot(xn, win_ref[:, d_ff + c * cf:d_ff + (c + 1) * cf])
        act = (_silu(gate) * up).astype(BF16)
        part = _dot(act, wout_ref[c * cf:(c + 1) * cf, :])
        if c == 0:
            acc_ref[...] = part
        else:
            acc_ref[...] += part
    o_ref[...] = x + FFN_HALF * _rmsnorm(acc_ref[...], gpost_ref[...])


def _ffn(h, norms3, w_in, w_out, layer, which, tm):
    t, d = h.shape
    d_ff = w_out.shape[2]
    n0 = layer * N_NORMS + (0 if which == 0 else 4)
    kern = functools.partial(_ffn_kernel, d_ff=d_ff, cf=256)
    return pl.pallas_call(
        kern,
        out_shape=jax.ShapeDtypeStruct((t, d), F32),
        grid=(t // tm,),
        in_specs=[
            pl.BlockSpec((tm, d), lambda i: (i, 0)),
            pl.BlockSpec((None, 1, d), lambda i: (n0, 0, 0)),
            pl.BlockSpec((None, 1, d), lambda i: (n0 + 1, 0, 0)),
            _resident((None, None, d, 2 * d_ff), lambda i: (layer, which, 0, 0)),
            _resident((None, None, d_ff, d), lambda i: (layer, which, 0, 0)),
        ],
        out_specs=pl.BlockSpec((tm, d), lambda i: (i, 0)),
        scratch_shapes=[pltpu.VMEM((tm, d), F32)],
        compiler_params=_params(("parallel",)),
        name="ffn",
    )(h, norms3, norms3, w_in, w_out)


def _in_proj_kernel(h_ref, g_ref, w_ref, z_ref, xbc_ref, dt_ref, *, d_inner, conv_dim, cn):
    u = _rmsnorm(h_ref[...], g_ref[...]).astype(BF16)
    for c in range(d_inner // cn):
        z_ref[:, c * cn:(c + 1) * cn] = _dot(u, w_ref[:, c * cn:(c + 1) * cn])
    for c in range(conv_dim // cn):
        xbc_ref[:, c * cn:(c + 1) * cn] = _dot(u, w_ref[:, d_inner + c * cn:d_inner + (c + 1) * cn])
    dt_ref[...] = _dot(u, w_ref[:, d_inner + conv_dim:])


def _mamba_in_proj(h, norms3, w, layer, j, d_inner, conv_dim, tm):
    t, d = h.shape
    n_all = w.shape[2]
    kern = functools.partial(_in_proj_kernel, d_inner=d_inner, conv_dim=conv_dim, cn=512)
    return pl.pallas_call(
        kern,
        out_shape=(jax.ShapeDtypeStruct((t, d_inner), F32),
                   jax.ShapeDtypeStruct((t, conv_dim), F32),
                   jax.ShapeDtypeStruct((t, M_DT_PAD), F32)),
        grid=(t // tm,),
        in_specs=[
            pl.BlockSpec((tm, d), lambda i: (i, 0)),
            pl.BlockSpec((None, 1, d), lambda i: (layer * N_NORMS + 2, 0, 0)),
            _resident((None, d, n_all), lambda i: (j, 0, 0)),
        ],
        out_specs=(pl.BlockSpec((tm, d_inner), lambda i: (i, 0)),
                   pl.BlockSpec((tm, conv_dim), lambda i: (i, 0)),
                   pl.BlockSpec((tm, M_DT_PAD), lambda i: (i, 0))),
        compiler_params=_params(("parallel",)),
        name="mamba_in_proj",
    )(h, norms3, w)


def _out_proj_kernel(y_ref, w_ref, h_ref, g_ref, o_ref):
    t = _dot(y_ref[...], w_ref[...])
    o_ref[...] = h_ref[...] + _rmsnorm(t, g_ref[...])


def _out_proj(y, w, j, h, norms3, layer, tm):
    t, d = h.shape
    k = y.shape[1]
    return pl.pallas_call(
        _out_proj_kernel,
        out_shape=jax.ShapeDtypeStruct((t, d), F32),
        grid=(t // tm,),
        in_specs=[
            pl.BlockSpec((tm, k), lambda i: (i, 0)),
            _resident((None, k, d), lambda i: (j, 0, 0)),
            pl.BlockSpec((tm, d), lambda i: (i, 0)),
            pl.BlockSpec((None, 1, d), lambda i: (layer * N_NORMS + 3, 0, 0)),
        ],
        out_specs=pl.BlockSpec((tm, d), lambda i: (i, 0)),
        compiler_params=_params(("parallel",)),
        name="out_proj",
    )(y, w, h, norms3)


def _ssd_kernel(z_ref, xbc_ref, dt_ref, cw_ref, cb_ref, dtb_ref, alog_ref, dskip_ref, nw_ref,
                yn_ref, xpad_ref, st_ref, *, q, d_inner):
    c = pl.program_id(1)
    gn = M_GROUPS * M_D_STATE

    @pl.when(c == 0)
    def _():
        xpad_ref[0:8, :] = jnp.zeros((8, xpad_ref.shape[1]), F32)
        st_ref[...] = jnp.zeros_like(st_ref)

    @pl.when(c > 0)
    def _():
        xpad_ref[0:8, :] = xpad_ref[q:q + 8, :]

    xpad_ref[8:q + 8, :] = xbc_ref[...]

    row = lax.broadcasted_iota(jnp.int32, (q, q), 0)
    col = lax.broadcasted_iota(jnp.int32, (q, q), 1)
    causal = row >= col
    tril = jnp.where(causal, 1.0, 0.0).astype(BF16)

    dtv = _softplus(dt_ref[...] + dtb_ref[...])
    dta = dtv * (-jnp.exp(alog_ref[...]))
    a_cum = _dot_exact_x(tril, dta)
    a_cum_t = a_cum.T
    ea = jnp.exp(a_cum)
    to_end = jnp.exp(a_cum[q - 1:q, :] - a_cum) * dtv
    dtv_hl, ea_hl, te_hl = _split(dtv), _split(ea), _split(to_end)

    erow = lax.broadcasted_iota(jnp.int32, (M_DT_PAD, M_GW), 0)
    ehead = lax.broadcasted_iota(jnp.int32, (M_DT_PAD, M_GW), 1) // M_HEADDIM
    lane_head = lax.broadcasted_iota(jnp.int32, (q, M_GW), 1) // M_HEADDIM

    def conv(c0, width):
        xa = xpad_ref[:, c0:c0 + width]
        acc = cb_ref[:, c0:c0 + width] + cw_ref[M_D_CONV - 1:M_D_CONV, c0:c0 + width] * xa[8:]
        for s in range(1, M_D_CONV):
            k = M_D_CONV - 1 - s
            acc = acc + cw_ref[k:k + 1, c0:c0 + width] * pltpu.roll(xa, s, 0)[8:]
        return _silu(acc)

    for g in range(M_GROUPS):
        expand = jnp.where(erow == M_HPG * g + ehead, 1.0, 0.0).astype(BF16)
        dt_x = _dot_split(dtv_hl, expand)
        ea_x = _dot_split(ea_hl, expand)
        te_x = _dot_split(te_hl, expand)

        xg = conv(g * M_GW, M_GW)
        bg = conv(d_inner + g * M_D_STATE, M_D_STATE)
        cg = conv(d_inner + gn + g * M_D_STATE, M_D_STATE)
        cg_bf = cg.astype(BF16)
        cbm = _dot_nt(cg_bf, bg.astype(BF16))

        xdt = (xg * dt_x).astype(BF16)
        w_heads, x_heads = [], []
        for e in range(M_HPG):
            hd = M_HPG * g + e
            seg = a_cum[:, hd:hd + 1] - a_cum_t[hd:hd + 1, :]
            dec = jnp.where(causal, jnp.exp(jnp.minimum(seg, 0.0)), 0.0)
            w_heads.append((cbm * dec).astype(BF16))
            x_heads.append(jnp.where(lane_head == e, xdt, jnp.zeros_like(xdt)))
        y = _dot(jnp.concatenate(w_heads, axis=1), jnp.concatenate(x_heads, axis=0))

        st = st_ref[g]
        y = y + _dot(cg_bf, st.astype(BF16)) * ea_x
        xte = (xg * te_x).astype(BF16)
        st_ref[g] = st * ea_x[q - 1:q, :] + _dot(bg.T.astype(BF16), xte)

        y = y + xg * dskip_ref[:, g * M_GW:(g + 1) * M_GW]
        y = y * _silu(z_ref[:, g * M_GW:(g + 1) * M_GW])
        ms = jnp.mean(y * y, axis=-1, keepdims=True)
        yn = y * lax.rsqrt(ms + M_NORM_EPS) * nw_ref[:, g * M_GW:(g + 1) * M_GW]
        yn_ref[:, g * M_GW:(g + 1) * M_GW] = yn.astype(BF16)


def _mamba_ssd(z, xbc, dt, conv_w, conv_b, dt_bias, a_log, d_skip, norm_w, j, batch, seq):
    t, d_inner = z.shape
    conv_dim = xbc.shape[1]
    q = min(SSD_Q, seq)
    nq = seq // q
    tok = lambda b, c: (b * nq + c, 0)
    par = lambda b, c: (j, 0, 0)
    kern = functools.partial(_ssd_kernel, q=q, d_inner=d_inner)
    return pl.pallas_call(
        kern,
        out_shape=jax.ShapeDtypeStruct((t, d_inner), BF16),
        grid=(batch, nq),
        in_specs=[
            pl.BlockSpec((q, d_inner), tok),
            pl.BlockSpec((q, conv_dim), tok),
            pl.BlockSpec((q, M_DT_PAD), tok),
            pl.BlockSpec((None, M_D_CONV, conv_dim), par),
            pl.BlockSpec((None, 1, conv_dim), par),
            pl.BlockSpec((None, 1, M_DT_PAD), par),
            pl.BlockSpec((None, 1, M_DT_PAD), par),
            pl.BlockSpec((None, 1, d_inner), par),
            pl.BlockSpec((None, 1, d_inner), par),
        ],
        out_specs=pl.BlockSpec((q, d_inner), tok),
        scratch_shapes=[pltpu.VMEM((q + 8, conv_dim), F32),
                        pltpu.VMEM((M_GROUPS, M_D_STATE, M_GW), F32)],
        compiler_params=_params(("parallel", "arbitrary")),
        name="mamba_ssd",
    )(z, xbc, dt, conv_w, conv_b, dt_bias, a_log, d_skip, norm_w)


def _head_sum(x, bd_ref):
    d = x.shape[1]
    cols = [_dot_x_exact(x[:, c * R_QUAD:(c + 1) * R_QUAD], bd_ref[...]) for c in range(d // R_QUAD)]
    return jnp.concatenate(cols, axis=1)


def _rwkv_project(h_ref, hp_ref, g_ref, mix_ref, wrkv_ref, w1_ref, w2_ref, a1_ref, a2_ref, g1_ref, g2_ref,
                  w0_ref, a0_ref, kk_ref, ka_ref, bd_ref, vres, seq_start,
                  r_ref, lw_ref, k_ref, v_ref, a_ref, b_ref, gate_ref):
    gn = g_ref[...]
    u = _rmsnorm(h_ref[...], gn)
    u_before = _rmsnorm(hp_ref[7:8, :], gn)
    u_before = jnp.where(seq_start, jnp.zeros_like(u_before), u_before)
    row = lax.broadcasted_iota(jnp.int32, u.shape, 0)
    u_prev = jnp.where(row == 0, u_before, pltpu.roll(u, 1, 0))
    delta = u_prev - u

    def shift(i):
        return (u + delta * mix_ref[i:i + 1, :]).astype(BF16)

    r_ref[...] = _dot(shift(0), wrkv_ref[0])
    k = _dot(shift(2), wrkv_ref[1])
    xv = shift(3)
    v = _dot(xv, wrkv_ref[2])
    if vres is not None:
        v0_ref, v1_ref, v2_ref, vf_ref = vres
        mixv = _sigmoid(v0_ref[...] + _dot(_dot(xv, v1_ref[...]).astype(BF16), v2_ref[...]))
        v = v + (vf_ref[...] - v) * mixv
    v_ref[...] = v
    wl = w0_ref[...] + _dot(jnp.tanh(_dot(shift(1), w1_ref[...])).astype(BF16), w2_ref[...])
    lw_ref[...] = -jnp.exp(-_softplus(-wl) - 0.5)
    gate_ref[...] = _dot(_sigmoid(_dot(shift(5), g1_ref[...])).astype(BF16), g2_ref[...])
    alpha = _sigmoid(a0_ref[...] + _dot(_dot(shift(4), a1_ref[...]).astype(BF16), a2_ref[...]))
    kk = k * kk_ref[...]
    kk = kk / jnp.maximum(jnp.sqrt(_head_sum(kk * kk, bd_ref)), R_L2_EPS)
    k_ref[...] = k * (1.0 + (alpha - 1.0) * ka_ref[...])
    a_ref[...] = -kk
    b_ref[...] = kk * alpha


def _rec_masks():
    li = np.arange(R_CHUNK)[:, None]
    lj = np.arange(R_CHUNK)[None, :]
    masks = [lj < li, lj <= li, (lj < li) & (li // 8 == lj // 8)]
    for m in (8, 16, 32):
        masks.append((li // (2 * m) == lj // (2 * m)) & (li % (2 * m) >= m) & (lj % (2 * m) < m))
    return jnp.asarray(np.stack([np.tile(m, (1, R_QUAD // R_HEAD)) for m in masks]).astype(np.float32))


def _rwkv_recurrence(r_ref, lw_ref, k_ref, v_ref, a_ref, b_ref, masks_ref, bd_ref, y_ref,
                     st_ref, qp_ref, g_ref, ct_ref, pe_ref, *, n_chunks, unroll):
    lc = R_CHUNK
    n_quads = r_ref.shape[1] // R_QUAD
    ti = lax.broadcasted_iota(jnp.int32, (lc, lc), 0)
    tj = lax.broadcasted_iota(jnp.int32, (lc, lc), 1)
    tril = jnp.where(ti >= tj, 1.0, 0.0).astype(BF16)
    lane_head = lax.broadcasted_iota(jnp.int32, (lc, R_QUAD), 1) // R_HEAD

    def quad(x):
        x = x.astype(BF16)
        zero = jnp.zeros_like(x)
        return jnp.concatenate([jnp.where(lane_head == hd, x, zero) for hd in range(R_QUAD // R_HEAD)], axis=0)

    each = lambda f, *ls: [f(*xs) for xs in zip(*ls)]
    hdot = lambda xs, ys: each(lambda x, y: _dot(x.astype(BF16), quad(y)), xs, ys)
    add3 = lambda xs, ys, zs: each(lambda x, y, z: x + y + z, xs, ys, zs)

    def tri_inverse_offdiag(n_mats):
        d1 = [n * masks_ref[2] for n in n_mats]
        d2 = hdot(d1, d1)
        d4 = hdot(d2, d2)
        xa = add3(d1, d2, hdot(d1, d2))
        x = add3(xa, d4, hdot(xa, d4))
        for lvl in range(3):
            nm = [n * masks_ref[3 + lvl] for n in n_mats]
            w = each(lambda p, q: p + q, nm, hdot(nm, x))
            x = add3(x, w, hdot(x, w))
        return x

    def build(insts):
        sls = [(pl.ds(pl.multiple_of(ci * lc, lc), lc), slice(q * R_QUAD, (q + 1) * R_QUAD)) for ci, q in insts]
        lw = [lw_ref[sl] for sl in sls]
        logp = [_dot_exact_x(tril, x) for x in lw]
        logp_end = [x[lc - 1:lc, :] for x in logp]
        pinv = [jnp.exp(-x) for x in logp]
        to_end = each(lambda e, x: jnp.exp(e - x), logp_end, logp)
        a_hat = [a_ref[sl] * jnp.exp(x - w) for sl, x, w in zip(sls, logp, lw)]
        q_un = [r_ref[sl] * jnp.exp(x) for sl, x in zip(sls, logp)]
        kc = [k_ref[sl] for sl in sls]
        bc = [b_ref[sl] for sl in sls]
        vc = [v_ref[sl] for sl in sls]
        lhs = each(lambda a, q: jnp.concatenate([a, q], axis=0).astype(BF16), a_hat, q_un)
        rhs = each(lambda b, k, pi: jnp.concatenate([quad(b * pi), quad(k * pi)], axis=0), bc, kc, pinv)
        a4 = each(_dot_nt, lhs, rhs)
        strict = masks_ref[0]
        incl = masks_ref[1]
        n_mat = [x[0:lc, 0:R_QUAD] * strict for x in a4]
        a_ak = [x[0:lc, R_QUAD:] * strict for x in a4]
        a_qb = [x[lc:, 0:R_QUAD] * incl for x in a4]
        a_qk = [(x[lc:, R_QUAD:] * incl).astype(BF16) for x in a4]

        v_q = [quad(x) for x in vc]
        z = each(lambda a, v: _dot(a.astype(BF16), v), a_ak, v_q)
        x = tri_inverse_offdiag(n_mat)
        xaz = each(lambda xx, a, zz: _dot(xx.astype(BF16), jnp.concatenate([quad(a), quad(zz)], axis=1)),
                   x, a_hat, z)
        ap = each(lambda a, d: a + d[:, 0:R_QUAD], a_hat, xaz)
        u0 = each(lambda zz, d: zz + d[:, R_QUAD:], z, xaz)
        qy = each(lambda m, a, u: _dot(m.astype(BF16), jnp.concatenate([quad(a), quad(u)], axis=1)),
                  a_qb, ap, u0)
        y0 = each(_dot, a_qk, v_q)
        bt = each(lambda b, e: (b * e).astype(BF16), bc, to_end)
        kt = each(lambda k, e: (k * e).astype(BF16), kc, to_end)
        g = each(lambda a, b: _dot_tn(a.astype(BF16), b), ap, bt)
        ct = each(lambda u, v, b, k: _dot_tn(jnp.concatenate([u.astype(BF16), v.astype(BF16)], axis=0),
                                             jnp.concatenate([b, k], axis=0)), u0, vc, bt, kt)
        same_head = bd_ref[...] > 0
        for i, (ci, q) in enumerate(insts):
            qp_ref[sls[i]] = (q_un[i] + qy[i][:, 0:R_QUAD]).astype(BF16)
            y_ref[sls[i]] = qy[i][:, R_QUAD:] + y0[i]
            g_ref[ci, q] = jnp.where(same_head, g[i], 0.0).astype(BF16)
            ct_ref[ci, q] = jnp.where(same_head, ct[i], 0.0)
            pe_ref[ci, :, q * R_QUAD:(q + 1) * R_QUAD] = jnp.broadcast_to(jnp.exp(logp_end[i]), (8, R_QUAD))

    def phase_a(i, carry):
        build([(i * unroll + u, q) for u in range(unroll) for q in range(n_quads)])
        return carry

    lax.fori_loop(0, n_chunks // unroll, phase_a, 0)

    def phase_b(ci, carry):
        rows = pl.ds(pl.multiple_of(ci * lc, lc), lc)
        lanes = [slice(q * R_QUAD, (q + 1) * R_QUAD) for q in range(n_quads)]
        st = [st_ref[q] for q in range(n_quads)]
        st_bf = [s.astype(BF16) for s in st]
        sg = [_dot(st_bf[q], g_ref[ci, q]) for q in range(n_quads)]
        yq = [_dot_nt(qp_ref[rows, lanes[q]], st_bf[q]) for q in range(n_quads)]
        for q in range(n_quads):
            st_ref[q] = st[q] * pe_ref[ci, 0:1, lanes[q]] + sg[q] + ct_ref[ci, q]
            y_ref[rows, lanes[q]] += yq[q]
        return carry

    lax.fori_loop(0, n_chunks, phase_b, 0)


def _rwkv_finish(y_ref, r_ref, k_ref, v_ref, gate_ref, lw_ref, lb_ref, rk_ref, bd_ref, wo_ref,
                 h_ref, g_ref, o_ref):
    y = y_ref[...]
    inv_n = 1.0 / R_HEAD
    mu = _head_sum(y, bd_ref) * inv_n
    yc = y - mu
    var = _head_sum(yc * yc, bd_ref) * inv_n
    yn = yc * lax.rsqrt(var + R_LNX_EPS) * lw_ref[...] + lb_ref[...]
    yn = yn + _head_sum(r_ref[...] * k_ref[...] * rk_ref[...], bd_ref) * v_ref[...]
    t = _dot((yn * gate_ref[...]).astype(BF16), wo_ref[...])
    o_ref[...] = h_ref[...] + _rmsnorm(t, g_ref[...])


def _rwkv_layer_kernel(*refs, has_vres, emit_v, n_chunks, unroll):
    (h_ref, hp_ref, gpre_ref, gpost_ref, mix_ref, wrkv_ref, w1_ref, w2_ref, a1_ref, a2_ref, g1_ref, g2_ref,
     w0_ref, a0_ref, kk_ref, ka_ref, bd_ref, lnw_ref, lnb_ref, rk_ref, wo_ref, masks_ref) = refs[:22]
    pos = 22
    vres = None
    if has_vres:
        vres = refs[pos:pos + 4]
        pos += 4
    o_ref = refs[pos]
    pos += 1
    if emit_v:
        v_ref = refs[pos]
        pos += 1
    (r_s, lw_s, k_s, a_s, b_s, gate_s, y_s) = refs[pos:pos + 7]
    pos += 7
    if not emit_v:
        v_ref = refs[pos]
        pos += 1
    st_s, qp_s, g_s, ct_s, pe_s = refs[pos:pos + 5]

    seq_start = pl.program_id(1) == 0

    @pl.when(seq_start)
    def _():
        st_s[...] = jnp.zeros_like(st_s)

    _rwkv_project(h_ref, hp_ref, gpre_ref, mix_ref, wrkv_ref, w1_ref, w2_ref, a1_ref, a2_ref, g1_ref, g2_ref,
                  w0_ref, a0_ref, kk_ref, ka_ref, bd_ref, vres, seq_start,
                  r_s, lw_s, k_s, v_ref, a_s, b_s, gate_s)
    _rwkv_recurrence(r_s, lw_s, k_s, v_ref, a_s, b_s, masks_ref, bd_ref, y_s,
                     st_s, qp_s, g_s, ct_s, pe_s, n_chunks=n_chunks, unroll=unroll)
    _rwkv_finish(y_s, r_s, k_s, v_ref, gate_s, lnw_ref, lnb_ref, rk_ref, bd_ref, wo_ref, h_ref, gpost_ref, o_ref)


def _rwkv_layer(h, norms3, layer, j, p, v_first, emit_v, batch, seq, lt):
    t, d = h.shape
    nt = seq // lt
    nch = lt // R_CHUNK
    nq = d // R_QUAD
    tok = lambda b, i: (b * nt + i, 0)
    prev = lambda b, i: (jnp.maximum((b * seq + i * lt) // 8 - 1, 0), 0)
    par3 = lambda b, i: (j, 0, 0)
    row = lambda: pl.BlockSpec((None, 1, d), par3)
    lp = R_LORA_PAD
    has_vres = v_first is not None
    masks = _rec_masks()
    in_specs = [
        pl.BlockSpec((lt, d), tok),
        pl.BlockSpec((8, d), prev),
        pl.BlockSpec((None, 1, d), lambda b, i: (layer * N_NORMS + 2, 0, 0)),
        pl.BlockSpec((None, 1, d), lambda b, i: (layer * N_NORMS + 3, 0, 0)),
        pl.BlockSpec((None, 8, d), par3),
        _resident((None, 3, d, d), lambda b, i: (j, 0, 0, 0)),
        _resident((None, d, lp), par3), _resident((None, lp, d), par3),
        _resident((None, d, lp), par3), _resident((None, lp, d), par3),
        _resident((None, d, lp), par3), _resident((None, lp, d), par3),
        row(), row(), row(), row(),
        _resident((R_QUAD, R_QUAD), lambda b, i: (0, 0)),
        row(), row(), row(),
        _resident((None, d, d), par3),
        _resident(masks.shape, lambda b, i: (0, 0, 0)),
    ]
    args = [h, h, norms3, norms3, p["mix"], p["w_rkv"], p["w1"], p["w2"], p["a1"], p["a2"], p["g1"], p["g2"],
            p["w0"], p["a0"], p["k_k"], p["k_a"], p["bd"], p["lnx_w"], p["lnx_b"], p["r_k"], p["w_o"], masks]
    if has_vres:
        parv = lambda b, i: (j - 1, 0, 0)
        in_specs += [pl.BlockSpec((None, 1, d), parv), _resident((None, d, lp), parv),
                     _resident((None, lp, d), parv), pl.BlockSpec((lt, d), tok)]
        args += [p["v0"], p["v1"], p["v2"], v_first]
    n_out = 2 if emit_v else 1
    scratch = [pltpu.VMEM((lt, d), F32) for _ in range(7 if emit_v else 8)]
    scratch += [pltpu.VMEM((nq, R_QUAD, R_QUAD), F32),
                pltpu.VMEM((lt, d), BF16),
                pltpu.VMEM((nch, nq, R_QUAD, R_QUAD), BF16),
                pltpu.VMEM((nch, nq, R_QUAD, R_QUAD), F32),
                pltpu.VMEM((nch, 8, d), F32)]
    kern = functools.partial(_rwkv_layer_kernel, has_vres=has_vres, emit_v=emit_v, n_chunks=nch,
                             unroll=2 if nch % 2 == 0 else 1)
    outs = pl.pallas_call(
        kern,
        out_shape=tuple(jax.ShapeDtypeStruct((t, d), F32) for _ in range(n_out)),
        grid=(batch, nt),
        in_specs=in_specs,
        out_specs=tuple(pl.BlockSpec((lt, d), tok) for _ in range(n_out)),
        scratch_shapes=scratch,
        compiler_params=_params(("parallel", "arbitrary")),
        name="rwkv_layer",
    )(*args)
    return outs if emit_v else (outs[0], None)


def _pad_last(x, n):
    return jnp.pad(x, [(0, 0)] * (x.ndim - 1) + [(0, n - x.shape[-1])])


def _pad_axis(x, axis, n):
    pads = [(0, 0)] * x.ndim
    pads[axis] = (0, n - x.shape[axis])
    return jnp.pad(x, pads)


def _row(x):
    return x.reshape(x.shape[0], 1, -1)


def kernel(x, norms, ffn_w_in, ffn_w_out, m_in_proj, m_conv_w, m_conv_b, m_dt_bias, m_A_log, m_D, m_norm_w, m_out_proj, r_mix, r_w_rkv, r_w_o, r_w0, r_w1, r_w2, r_a0, r_a1, r_a2, r_v0, r_v1, r_v2, r_g1, r_g2, r_k_k, r_k_a, r_r_k, r_lnx_w, r_lnx_b):
    batch, seq, d = x.shape
    depth = norms.shape[0]
    t = batch * seq
    d_inner = m_out_proj.shape[1]
    conv_dim = m_conv_w.shape[2]
    m_heads = m_dt_bias.shape[1]
    assert m_heads <= M_DT_PAD and d_inner == M_GROUPS * M_GW and d % R_QUAD == 0

    tm_ffn = min(512, t)
    tm_in = min(256, t)
    tm_out = min(512, t)
    lt_rwkv = min(256, seq)

    norms3 = _row(norms.reshape(depth * N_NORMS, d))
    w_in = ffn_w_in.astype(BF16)
    w_out = ffn_w_out.astype(BF16)
    n_zx = d_inner + conv_dim
    m_w = jnp.concatenate([m_in_proj[:, :, :n_zx], _pad_last(m_in_proj[:, :, n_zx:], M_DT_PAD)], axis=2).astype(BF16)
    m_wo = m_out_proj.astype(BF16)
    m_cb = _row(m_conv_b)
    m_dtb = _row(_pad_last(m_dt_bias, M_DT_PAD))
    m_al = _row(_pad_last(m_A_log, M_DT_PAD))
    m_dsk = _row(jnp.repeat(m_D, M_HEADDIM, axis=1))
    m_nw = _row(m_norm_w)

    lp = R_LORA_PAD
    blk = np.arange(R_QUAD) // R_HEAD
    rp = {
        "mix": _pad_axis(r_mix, 1, 8),
        "w_rkv": r_w_rkv.astype(BF16), "w_o": r_w_o.astype(BF16),
        "w1": _pad_last(r_w1, lp).astype(BF16), "w2": _pad_axis(r_w2, 1, lp).astype(BF16),
        "a1": _pad_last(r_a1, lp).astype(BF16), "a2": _pad_axis(r_a2, 1, lp).astype(BF16),
        "g1": _pad_last(r_g1, lp).astype(BF16), "g2": _pad_axis(r_g2, 1, lp).astype(BF16),
        "w0": _row(r_w0), "a0": _row(r_a0), "k_k": _row(r_k_k), "k_a": _row(r_k_a),
        "r_k": _row(r_r_k.reshape(r_r_k.shape[0], -1)), "lnx_w": _row(r_lnx_w), "lnx_b": _row(r_lnx_b),
        "bd": jnp.asarray((blk[:, None] == blk[None, :]).astype(np.float32), dtype=BF16),
    }
    if r_v0.shape[0] > 0:
        rp.update({"v0": _row(r_v0), "v1": _pad_last(r_v1, lp).astype(BF16),
                   "v2": _pad_axis(r_v2, 1, lp).astype(BF16)})

    h = x.reshape(t, d)
    v_first = None
    for i in range(depth):
        j = i // 2
        h = _ffn(h, norms3, w_in, w_out, i, 0, tm_ffn)
        if i % 2 == 0:
            z, xbc, dt = _mamba_in_proj(h, norms3, m_w, i, j, d_inner, conv_dim, tm_in)
            yn = _mamba_ssd(z, xbc, dt, m_conv_w, m_cb, m_dtb, m_al, m_dsk, m_nw, j, batch, seq)
            h = _out_proj(yn, m_wo, j, h, norms3, i, tm_out)
        else:
            emit_v = j == 0 and depth // 2 > 1
            h, v_layer = _rwkv_layer(h, norms3, i, j, rp, v_first if j > 0 else None, emit_v,
                                     batch, seq, lt_rwkv)
            if j == 0:
                v_first = v_layer
        h = _ffn(h, norms3, w_in, w_out, i, 1, tm_ffn)
    return h.reshape(batch, seq, d)
```

```python
import functools

import numpy as np
import jax
import jax.numpy as jnp
from jax import lax
from jax.experimental import pallas as pl
from jax.experimental.pallas import tpu as pltpu

F32 = jnp.float32
BF16 = jnp.bfloat16

NORM_EPS = 1e-6
N_NORMS = 6
FFN_HALF = 0.5

M_HEADDIM = 64
M_GROUPS = 8
M_HPG = 4
M_D_STATE = 128
M_D_CONV = 4
M_NORM_EPS = 1e-5
M_GW = M_HPG * M_HEADDIM
M_DT_PAD = 128
SSD_Q = 128

R_HEAD = 64
R_LNX_EPS = 64e-5
R_L2_EPS = 1e-12
R_LORA_PAD = 128
R_CHUNK = 64
R_QUAD = 4 * R_HEAD

VMEM_LIMIT = 56 * 1024 * 1024


def _rmsnorm(x, g, eps=NORM_EPS):
    return x * lax.rsqrt(jnp.mean(x * x, axis=-1, keepdims=True) + eps) * g


def _sigmoid(x):
    return 0.5 + 0.5 * jnp.tanh(0.5 * x)


def _silu(x):
    half = 0.5 * x
    return half + half * jnp.tanh(half)


def _softplus(x):
    return jnp.maximum(x, 0.0) + jnp.log(1.0 + jnp.exp(-jnp.abs(x)))


def _dot(a, b):
    return jnp.dot(a, b, preferred_element_type=F32)


def _dot_nt(a, b):
    return lax.dot_general(a, b, (((1,), (1,)), ((), ())), preferred_element_type=F32)


def _dot_tn(a, b):
    return lax.dot_general(a, b, (((0,), (0,)), ((), ())), preferred_element_type=F32)


def _split(x):
    hi = x.astype(BF16)
    lo = (x - hi.astype(F32)).astype(BF16)
    return hi, lo


def _dot_split(hl, m):
    return _dot(hl[0], m) + _dot(hl[1], m)


def _dot_x_exact(x, m):
    return _dot_split(_split(x), m)


def _dot_exact_x(m, x):
    hi, lo = _split(x)
    return _dot(m, hi) + _dot(m, lo)


def _params(sem):
    return pltpu.CompilerParams(dimension_semantics=sem, vmem_limit_bytes=VMEM_LIMIT)


def _resident(block, index_map):
    return pl.BlockSpec(block, index_map, pipeline_mode=pl.Buffered(1))


def _ffn_kernel(x_ref, gpre_ref, gpost_ref, win_ref, wout_ref, o_ref, acc_ref, *, d_ff, cf):
    x = x_ref[...]
    xn = _rmsnorm(x, gpre_ref[...]).astype(BF16)
    for c in range(d_ff // cf):
        gate = _dot(xn, win_ref[:, c * cf:(c + 1) * cf])
        up = _dot(xn, win_ref[:, d_ff + c * cf:d_ff + (c + 1) * cf])
        act = (_silu(gate) * up).astype(BF16)
        part = _dot(act, wout_ref[c * cf:(c + 1) * cf, :])
        if c == 0:
            acc_ref[...] = part
        else:
            acc_ref[...] += part
    o_ref[...] = x + FFN_HALF * _rmsnorm(acc_ref[...], gpost_ref[...])


def _ffn(h, norms3, w_in, w_out, layer, which, tm):
    t, d = h.shape
    d_ff = w_out.shape[2]
    n0 = layer * N_NORMS + (0 if which == 0 else 4)
    kern = functools.partial(_ffn_kernel, d_ff=d_ff, cf=256)
    return pl.pallas_call(
        kern,
        out_shape=jax.ShapeDtypeStruct((t, d), F32),
        grid=(t // tm,),
        in_specs=[
            pl.BlockSpec((tm, d), lambda i: (i, 0)),
            pl.BlockSpec((None, 1, d), lambda i: (n0, 0, 0)),
            pl.BlockSpec((None, 1, d), lambda i: (n0 + 1, 0, 0)),
            _resident((None, None, d, 2 * d_ff), lambda i: (layer, which, 0, 0)),
            _resident((None, None, d_ff, d), lambda i: (layer, which, 0, 0)),
        ],
        out_specs=pl.BlockSpec((tm, d), lambda i: (i, 0)),
        scratch_shapes=[pltpu.VMEM((tm, d), F32)],
        compiler_params=_params(("parallel",)),
        name="ffn",
    )(h, norms3, norms3, w_in, w_out)


def _mamba_layer_kernel(h_ref, gpre_ref, gpost_ref, win_ref, wo_ref, cw_ref, cb_ref, dtb_ref, alog_ref,
                        dskip_ref, nw_ref, o_ref, xpad_ref, z_ref, yn_ref, st_ref, *, lt, q):
    c = pl.program_id(1)
    gcols = M_GW + 2 * M_D_STATE
    n_sub = lt // q

    @pl.when(c == 0)
    def _():
        xpad_ref[0:8, :] = jnp.zeros((8, xpad_ref.shape[1]), F32)
        st_ref[...] = jnp.zeros_like(st_ref)

    @pl.when(c > 0)
    def _():
        xpad_ref[0:8, :] = xpad_ref[lt:lt + 8, :]

    h = h_ref[...]
    u = _rmsnorm(h, gpre_ref[...]).astype(BF16)
    n_grp = M_GROUPS * (M_GW + gcols)
    dt_all = _dot(u, win_ref[:, n_grp:])

    row = lax.broadcasted_iota(jnp.int32, (q, q), 0)
    col = lax.broadcasted_iota(jnp.int32, (q, q), 1)
    causal = row >= col
    tril = jnp.where(causal, 1.0, 0.0).astype(BF16)
    neg_a = -jnp.exp(alog_ref[...])
    pre = []
    for s in range(n_sub):
        dtv = _softplus(dt_all[s * q:(s + 1) * q] + dtb_ref[...])
        a_cum = _dot_exact_x(tril, dtv * neg_a)
        ea = jnp.exp(a_cum)
        to_end = jnp.exp(a_cum[q - 1:q, :] - a_cum) * dtv
        pre.append((a_cum, a_cum.T, dtv.astype(BF16), ea.astype(BF16), to_end.astype(BF16), ea[q - 8:q]))

    erow = lax.broadcasted_iota(jnp.int32, (M_DT_PAD, M_GW), 0)
    ehead = lax.broadcasted_iota(jnp.int32, (M_DT_PAD, M_GW), 1) // M_HEADDIM
    lane_head = lax.broadcasted_iota(jnp.int32, (q, M_GW), 1) // M_HEADDIM

    def project(g):
        w0 = g * (M_GW + gcols)
        res = _dot(u, win_ref[:, w0:w0 + M_GW + gcols])
        z_ref[:, g * M_GW:(g + 1) * M_GW] = res[:, 0:M_GW]
        xpad_ref[8:8 + lt, g * gcols:(g + 1) * gcols] = res[:, M_GW:]

    def ssd(s, g):
        a_cum, a_cum_t, dtv_bf, ea_bf, te_bf, ea_tail = pre[s]
        r0 = s * q
        cols = slice(g * gcols, (g + 1) * gcols)
        xa = xpad_ref[r0:r0 + q + 8, cols]
        acc = cb_ref[:, cols] + cw_ref[M_D_CONV - 1:M_D_CONV, cols] * xa[8:]
        for sh in range(1, M_D_CONV):
            k = M_D_CONV - 1 - sh
            acc = acc + cw_ref[k:k + 1, cols] * pltpu.roll(xa, sh, 0)[8:]
        xbc = _silu(acc)
        xg = xbc[:, 0:M_GW]
        bg = xbc[:, M_GW:M_GW + M_D_STATE]
        cg_bf = xbc[:, M_GW + M_D_STATE:].astype(BF16)

        expand = jnp.where(erow == M_HPG * g + ehead, 1.0, 0.0).astype(BF16)
        dt_x = _dot(dtv_bf, expand)
        ea_x = _dot(ea_bf, expand)
        te_x = _dot(te_bf, expand)
        ea_end = _dot_x_exact(ea_tail, expand)[7:8, :]

        cbm = _dot_nt(cg_bf, bg.astype(BF16))
        xdt = (xg * dt_x).astype(BF16)
        w_heads, x_heads = [], []
        for e in range(M_HPG):
            hd = M_HPG * g + e
            seg = a_cum[:, hd:hd + 1] - a_cum_t[hd:hd + 1, :]
            dec = jnp.where(causal, jnp.exp(jnp.minimum(seg, 0.0)), 0.0)
            w_heads.append((cbm * dec).astype(BF16))
            x_heads.append(jnp.where(lane_head == e, xdt, jnp.zeros_like(xdt)))
        y = _dot(jnp.concatenate(w_heads, axis=1), jnp.concatenate(x_heads, axis=0))

        st = st_ref[g]
        y = y + _dot(cg_bf, st.astype(BF16)) * ea_x
        xte = (xg * te_x).astype(BF16)
        st_ref[g] = st * ea_end + _dot(bg.T.astype(BF16), xte)

        gl = slice(g * M_GW, (g + 1) * M_GW)
        y = y + xg * dskip_ref[:, gl]
        y = y * _silu(z_ref[r0:r0 + q, gl])
        ms = jnp.mean(y * y, axis=-1, keepdims=True)
        yn_ref[r0:r0 + q, gl] = (y * lax.rsqrt(ms + M_NORM_EPS) * nw_ref[:, gl]).astype(BF16)

    def out_partial(g):
        return _dot(yn_ref[:, g * M_GW:(g + 1) * M_GW], wo_ref[g * M_GW:(g + 1) * M_GW, :])

    project(0)
    for g in range(M_GROUPS):
        if g + 1 < M_GROUPS:
            project(g + 1)
        ssd(0, g)
    for s in range(1, n_sub - 1):
        for g in range(M_GROUPS):
            ssd(s, g)
    t = None
    for g in range(M_GROUPS):
        if n_sub > 1:
            ssd(n_sub - 1, g)
        part = out_partial(g)
        t = part if t is None else t + part
    o_ref[...] = h + _rmsnorm(t, gpost_ref[...])


def _mamba_layer(h, norms3, w_in, w_out, conv_w, conv_b, dt_bias, a_log, d_skip, norm_w, layer, j,
                 batch, seq, lt):
    t, d = h.shape
    d_inner = w_out.shape[1]
    conv_dim = conv_w.shape[2]
    q = min(SSD_Q, lt)
    nt = seq // lt
    tok = lambda b, c: (b * nt + c, 0)
    par = lambda b, c: (j, 0, 0)
    kern = functools.partial(_mamba_layer_kernel, lt=lt, q=q)
    return pl.pallas_call(
        kern,
        out_shape=jax.ShapeDtypeStruct((t, d), F32),
        grid=(batch, nt),
        in_specs=[
            pl.BlockSpec((lt, d), tok),
            pl.BlockSpec((None, 1, d), lambda b, c: (layer * N_NORMS + 2, 0, 0)),
            pl.BlockSpec((None, 1, d), lambda b, c: (layer * N_NORMS + 3, 0, 0)),
            _resident((None, d, w_in.shape[2]), par),
            _resident((None, d_inner, d), par),
            pl.BlockSpec((None, M_D_CONV, conv_dim), par),
            pl.BlockSpec((None, 1, conv_dim), par),
            pl.BlockSpec((None, 1, M_DT_PAD), par),
            pl.BlockSpec((None, 1, M_DT_PAD), par),
            pl.BlockSpec((None, 1, d_inner), par),
            pl.BlockSpec((None, 1, d_inner), par),
        ],
        out_specs=pl.BlockSpec((lt, d), tok),
        scratch_shapes=[pltpu.VMEM((lt + 8, conv_dim), F32),
                        pltpu.VMEM((lt, d_inner), F32),
                        pltpu.VMEM((lt, d_inner), BF16),
                        pltpu.VMEM((M_GROUPS, M_D_STATE, M_GW), F32)],
        compiler_params=_params(("parallel", "arbitrary")),
        name="mamba_layer",
    )(h, norms3, norms3, w_in, w_out, conv_w, conv_b, dt_bias, a_log, d_skip, norm_w)


def _head_sum(x, bd_ref):
    d = x.shape[1]
    cols = [_dot(x[:, c * R_QUAD:(c + 1) * R_QUAD].astype(BF16), bd_ref[...]) for c in range(d // R_QUAD)]
    return jnp.concatenate(cols, axis=1)


def _rwkv_project(h_ref, hp_ref, g_ref, mix_ref, wrkv_ref, w1_ref, w2_ref, a1_ref, a2_ref, g1_ref, g2_ref,
                  w0_ref, a0_ref, kk_ref, ka_ref, bd_ref, vres, seq_start,
                  r_ref, lw_ref, k_ref, v_ref, a_ref, b_ref, gate_ref):
    gn = g_ref[...]
    u = _rmsnorm(h_ref[...], gn)
    u_before = _rmsnorm(hp_ref[7:8, :], gn)
    u_before = jnp.where(seq_start, jnp.zeros_like(u_before), u_before)
    row = lax.broadcasted_iota(jnp.int32, u.shape, 0)
    u_prev = jnp.where(row == 0, u_before, pltpu.roll(u, 1, 0))
    delta = u_prev - u

    def shift(i):
        return (u + delta * mix_ref[i:i + 1, :]).astype(BF16)

    r_ref[...] = _dot(shift(0), wrkv_ref[0])
    k = _dot(shift(2), wrkv_ref[1])
    xv = shift(3)
    v = _dot(xv, wrkv_ref[2])
    if vres is not None:
        v0_ref, v1_ref, v2_ref, vf_ref = vres
        mixv = _sigmoid(v0_ref[...] + _dot(_dot(xv, v1_ref[...]).astype(BF16), v2_ref[...]))
        v = v + (vf_ref[...] - v) * mixv
    v_ref[...] = v
    wl = w0_ref[...] + _dot(jnp.tanh(_dot(shift(1), w1_ref[...])).astype(BF16), w2_ref[...])
    lw_ref[...] = -jnp.exp(-_softplus(-wl) - 0.5)
    gate_ref[...] = _dot(_sigmoid(_dot(shift(5), g1_ref[...])).astype(BF16), g2_ref[...])
    alpha = _sigmoid(a0_ref[...] + _dot(_dot(shift(4), a1_ref[...]).astype(BF16), a2_ref[...]))
    kk = k * kk_ref[...]
    kk = kk / jnp.maximum(jnp.sqrt(_head_sum(kk * kk, bd_ref)), R_L2_EPS)
    k_ref[...] = k * (1.0 + (alpha - 1.0) * ka_ref[...])
    a_ref[...] = -kk
    b_ref[...] = kk * alpha


def _rec_masks():
    li = np.arange(R_CHUNK)[:, None]
    lj = np.arange(R_CHUNK)[None, :]
    masks = [lj < li, lj <= li, (lj < li) & (li // 8 == lj // 8)]
    for m in (8, 16, 32):
        masks.append((li // (2 * m) == lj // (2 * m)) & (li % (2 * m) >= m) & (lj % (2 * m) < m))
    return jnp.asarray(np.stack([np.tile(m, (1, R_QUAD // R_HEAD)) for m in masks]).astype(np.float32))


def _rwkv_recurrence(r_ref, lw_ref, k_ref, v_ref, a_ref, b_ref, masks_ref, bd_ref, y_ref,
                     st_ref, qp_ref, g_ref, ct_ref, pe_ref, *, n_chunks, unroll):
    lc = R_CHUNK
    n_quads = r_ref.shape[1] // R_QUAD
    ti = lax.broadcasted_iota(jnp.int32, (lc, lc), 0)
    tj = lax.broadcasted_iota(jnp.int32, (lc, lc), 1)
    tril = jnp.where(ti >= tj, 1.0, 0.0).astype(BF16)
    lane_head = lax.broadcasted_iota(jnp.int32, (lc, R_QUAD), 1) // R_HEAD

    def quad(x):
        x = x.astype(BF16)
        zero = jnp.zeros_like(x)
        return jnp.concatenate([jnp.where(lane_head == hd, x, zero) for hd in range(R_QUAD // R_HEAD)], axis=0)

    each = lambda f, *ls: [f(*xs) for xs in zip(*ls)]
    hdot = lambda xs, ys: each(lambda x, y: _dot(x.astype(BF16), quad(y)), xs, ys)
    add3 = lambda xs, ys, zs: each(lambda x, y, z: x + y + z, xs, ys, zs)

    def tri_inverse_offdiag(n_mats):
        d1 = [n * masks_ref[2] for n in n_mats]
        d2 = hdot(d1, d1)
        d4 = hdot(d2, d2)
        xa = add3(d1, d2, hdot(d1, d2))
        x = add3(xa, d4, hdot(xa, d4))
        for lvl in range(3):
            nm = [n * masks_ref[3 + lvl] for n in n_mats]
            w = each(lambda p, q: p + q, nm, hdot(nm, x))
            x = add3(x, w, hdot(x, w))
        return x

    def build(insts):
        sls = [(pl.ds(pl.multiple_of(ci * lc, lc), lc), slice(q * R_QUAD, (q + 1) * R_QUAD)) for ci, q in insts]
        lw = [lw_ref[sl] for sl in sls]
        logp = [_dot_exact_x(tril, x) for x in lw]
        logp_end = [x[lc - 1:lc, :] for x in logp]
        pinv = [jnp.exp(-x) for x in logp]
        to_end = each(lambda e, x: jnp.exp(e - x), logp_end, logp)
        a_hat = [a_ref[sl] * jnp.exp(x - w) for sl, x, w in zip(sls, logp, lw)]
        q_un = [r_ref[sl] * jnp.exp(x) for sl, x in zip(sls, logp)]
        kc = [k_ref[sl] for sl in sls]
        bc = [b_ref[sl] for sl in sls]
        vc = [v_ref[sl] for sl in sls]
        lhs = each(lambda a, q: jnp.concatenate([a, q], axis=0).astype(BF16), a_hat, q_un)
        rhs = each(lambda b, k, pi: jnp.concatenate([quad(b * pi), quad(k * pi)], axis=0), bc, kc, pinv)
        a4 = each(_dot_nt, lhs, rhs)
        strict = masks_ref[0]
        incl = masks_ref[1]
        n_mat = [x[0:lc, 0:R_QUAD] * strict for x in a4]
        a_ak = [x[0:lc, R_QUAD:] * strict for x in a4]
        a_qb = [x[lc:, 0:R_QUAD] * incl for x in a4]
        a_qk = [(x[lc:, R_QUAD:] * incl).astype(BF16) for x in a4]

        v_q = [quad(x) for x in vc]
        z = each(lambda a, v: _dot(a.astype(BF16), v), a_ak, v_q)
        x = tri_inverse_offdiag(n_mat)
        xaz = each(lambda xx, a, zz: _dot(xx.astype(BF16), jnp.concatenate([quad(a), quad(zz)], axis=1)),
                   x, a_hat, z)
        ap = each(lambda a, d: a + d[:, 0:R_QUAD], a_hat, xaz)
        u0 = each(lambda zz, d: zz + d[:, R_QUAD:], z, xaz)
        qy = each(lambda m, a, u: _dot(m.astype(BF16), jnp.concatenate([quad(a), quad(u)], axis=1)),
                  a_qb, ap, u0)
        y0 = each(_dot, a_qk, v_q)
        bt = each(lambda b, e: (b * e).astype(BF16), bc, to_end)
        kt = each(lambda k, e: (k * e).astype(BF16), kc, to_end)
        g = each(lambda a, b: _dot_tn(a.astype(BF16), b), ap, bt)
        ct = each(lambda u, v, b, k: _dot_tn(jnp.concatenate([u.astype(BF16), v.astype(BF16)], axis=0),
                                             jnp.concatenate([b, k], axis=0)), u0, vc, bt, kt)
        same_head = bd_ref[...] > 0
        for i, (ci, q) in enumerate(insts):
            qp_ref[sls[i]] = (q_un[i] + qy[i][:, 0:R_QUAD]).astype(BF16)
            y_ref[sls[i]] = qy[i][:, R_QUAD:] + y0[i]
            g_ref[ci, q] = jnp.where(same_head, g[i], 0.0).astype(BF16)
            ct_ref[ci, q] = jnp.where(same_head, ct[i], 0.0)
            pe_ref[ci, :, q * R_QUAD:(q + 1) * R_QUAD] = jnp.broadcast_to(jnp.exp(logp_end[i]), (8, R_QUAD))

    def phase_a(i, carry):
        build([(i * unroll + u, q) for u in range(unroll) for q in range(n_quads)])
        return carry

    lax.fori_loop(0, n_chunks // unroll, phase_a, 0)

    def phase_b(ci, carry):
        rows = pl.ds(pl.multiple_of(ci * lc, lc), lc)
        lanes = [slice(q * R_QUAD, (q + 1) * R_QUAD) for q in range(n_quads)]
        st = [st_ref[q] for q in range(n_quads)]
        st_bf = [s.astype(BF16) for s in st]
        sg = [_dot(st_bf[q], g_ref[ci, q]) for q in range(n_quads)]
        yq = [_dot_nt(qp_ref[rows, lanes[q]], st_bf[q]) for q in range(n_quads)]
        for q in range(n_quads):
            st_ref[q] = st[q] * pe_ref[ci, 0:1, lanes[q]] + sg[q] + ct_ref[ci, q]
            y_ref[rows, lanes[q]] += yq[q]
        return carry

    lax.fori_loop(0, n_chunks, phase_b, 0)


def _rwkv_finish(y_ref, r_ref, k_ref, v_ref, gate_ref, lw_ref, lb_ref, rk_ref, bd_ref, wo_ref,
                 h_ref, g_ref, o_ref):
    y = y_ref[...]
    inv_n = 1.0 / R_HEAD
    mu = _head_sum(y, bd_ref) * inv_n
    yc = y - mu
    var = _head_sum(yc * yc, bd_ref) * inv_n
    yn = yc * lax.rsqrt(var + R_LNX_EPS) * lw_ref[...] + lb_ref[...]
    yn = yn + _head_sum(r_ref[...] * k_ref[...] * rk_ref[...], bd_ref) * v_ref[...]
    t = _dot((yn * gate_ref[...]).astype(BF16), wo_ref[...])
    o_ref[...] = h_ref[...] + _rmsnorm(t, g_ref[...])


def _rwkv_layer_kernel(*refs, has_vres, emit_v, n_chunks, unroll):
    (h_ref, hp_ref, gpre_ref, gpost_ref, mix_ref, wrkv_ref, w1_ref, w2_ref, a1_ref, a2_ref, g1_ref, g2_ref,
     w0_ref, a0_ref, kk_ref, ka_ref, bd_ref, lnw_ref, lnb_ref, rk_ref, wo_ref, masks_ref) = refs[:22]
    pos = 22
    vres = None
    if has_vres:
        vres = refs[pos:pos + 4]
        pos += 4
    o_ref = refs[pos]
    pos += 1
    if emit_v:
        v_ref = refs[pos]
        pos += 1
    (r_s, lw_s, k_s, a_s, b_s, gate_s, y_s) = refs[pos:pos + 7]
    pos += 7
    if not emit_v:
        v_ref = refs[pos]
        pos += 1
    st_s, qp_s, g_s, ct_s, pe_s = refs[pos:pos + 5]

    seq_start = pl.program_id(1) == 0

    @pl.when(seq_start)
    def _():
        st_s[...] = jnp.zeros_like(st_s)

    _rwkv_project(h_ref, hp_ref, gpre_ref, mix_ref, wrkv_ref, w1_ref, w2_ref, a1_ref, a2_ref, g1_ref, g2_ref,
                  w0_ref, a0_ref, kk_ref, ka_ref, bd_ref, vres, seq_start,
                  r_s, lw_s, k_s, v_ref, a_s, b_s, gate_s)
    _rwkv_recurrence(r_s, lw_s, k_s, v_ref, a_s, b_s, masks_ref, bd_ref, y_s,
                     st_s, qp_s, g_s, ct_s, pe_s, n_chunks=n_chunks, unroll=unroll)
    _rwkv_finish(y_s, r_s, k_s, v_ref, gate_s, lnw_ref, lnb_ref, rk_ref, bd_ref, wo_ref, h_ref, gpost_ref, o_ref)


def _rwkv_layer(h, norms3, layer, j, p, v_first, emit_v, batch, seq, lt):
    t, d = h.shape
    nt = seq // lt
    nch = lt // R_CHUNK
    nq = d // R_QUAD
    tok = lambda b, i: (b * nt + i, 0)
    prev = lambda b, i: (jnp.maximum((b * seq + i * lt) // 8 - 1, 0), 0)
    par3 = lambda b, i: (j, 0, 0)
    row = lambda: pl.BlockSpec((None, 1, d), par3)
    lp = R_LORA_PAD
    has_vres = v_first is not None
    masks = _rec_masks()
    in_specs = [
        pl.BlockSpec((lt, d), tok),
        pl.BlockSpec((8, d), prev),
        pl.BlockSpec((None, 1, d), lambda b, i: (layer * N_NORMS + 2, 0, 0)),
        pl.BlockSpec((None, 1, d), lambda b, i: (layer * N_NORMS + 3, 0, 0)),
        pl.BlockSpec((None, 8, d), par3),
        _resident((None, 3, d, d), lambda b, i: (j, 0, 0, 0)),
        _resident((None, d, lp), par3), _resident((None, lp, d), par3),
        _resident((None, d, lp), par3), _resident((None, lp, d), par3),
        _resident((None, d, lp), par3), _resident((None, lp, d), par3),
        row(), row(), row(), row(),
        _resident((R_QUAD, R_QUAD), lambda b, i: (0, 0)),
        row(), row(), row(),
        _resident((None, d, d), par3),
        _resident(masks.shape, lambda b, i: (0, 0, 0)),
    ]
    args = [h, h, norms3, norms3, p["mix"], p["w_rkv"], p["w1"], p["w2"], p["a1"], p["a2"], p["g1"], p["g2"],
            p["w0"], p["a0"], p["k_k"], p["k_a"], p["bd"], p["lnx_w"], p["lnx_b"], p["r_k"], p["w_o"], masks]
    if has_vres:
        parv = lambda b, i: (j - 1, 0, 0)
        in_specs += [pl.BlockSpec((None, 1, d), parv), _resident((None, d, lp), parv),
                     _resident((None, lp, d), parv), pl.BlockSpec((lt, d), tok)]
        args += [p["v0"], p["v1"], p["v2"], v_first]
    n_out = 2 if emit_v else 1
    scratch = [pltpu.VMEM((lt, d), F32) for _ in range(7 if emit_v else 8)]
    scratch += [pltpu.VMEM((nq, R_QUAD, R_QUAD), F32),
                pltpu.VMEM((lt, d), BF16),
                pltpu.VMEM((nch, nq, R_QUAD, R_QUAD), BF16),
                pltpu.VMEM((nch, nq, R_QUAD, R_QUAD), F32),
                pltpu.VMEM((nch, 8, d), F32)]
    kern = functools.partial(_rwkv_layer_kernel, has_vres=has_vres, emit_v=emit_v, n_chunks=nch,
                             unroll=4 if nch % 4 == 0 else 1)
    outs = pl.pallas_call(
        kern,
        out_shape=tuple(jax.ShapeDtypeStruct((t, d), F32) for _ in range(n_out)),
        grid=(batch, nt),
        in_specs=in_specs,
        out_specs=tuple(pl.BlockSpec((lt, d), tok) for _ in range(n_out)),
        scratch_shapes=scratch,
        compiler_params=_params(("parallel", "arbitrary")),
        name="rwkv_layer",
    )(*args)
    return outs if emit_v else (outs[0], None)


def _pad_last(x, n):
    return jnp.pad(x, [(0, 0)] * (x.ndim - 1) + [(0, n - x.shape[-1])])


def _pad_axis(x, axis, n):
    pads = [(0, 0)] * x.ndim
    pads[axis] = (0, n - x.shape[axis])
    return jnp.pad(x, pads)


def _row(x):
    return x.reshape(x.shape[0], 1, -1)


def kernel(x, norms, ffn_w_in, ffn_w_out, m_in_proj, m_conv_w, m_conv_b, m_dt_bias, m_A_log, m_D, m_norm_w, m_out_proj, r_mix, r_w_rkv, r_w_o, r_w0, r_w1, r_w2, r_a0, r_a1, r_a2, r_v0, r_v1, r_v2, r_g1, r_g2, r_k_k, r_k_a, r_r_k, r_lnx_w, r_lnx_b):
    batch, seq, d = x.shape
    depth = norms.shape[0]
    t = batch * seq
    d_inner = m_out_proj.shape[1]
    conv_dim = m_conv_w.shape[2]
    m_heads = m_dt_bias.shape[1]
    assert m_heads <= M_DT_PAD and d_inner == M_GROUPS * M_GW and d % R_QUAD == 0
    assert conv_dim == d_inner + 2 * M_GROUPS * M_D_STATE

    tm_ffn = min(512, t)
    lt_mamba = min(256, seq)
    lt_rwkv = min(256, seq)

    norms3 = _row(norms.reshape(depth * N_NORMS, d))
    w_in = ffn_w_in.astype(BF16)
    w_out = ffn_w_out.astype(BF16)
    gn = M_GROUPS * M_D_STATE
    xbc_cols = np.concatenate([np.concatenate([np.arange(g * M_GW, (g + 1) * M_GW),
                                               d_inner + np.arange(g * M_D_STATE, (g + 1) * M_D_STATE),
                                               d_inner + gn + np.arange(g * M_D_STATE, (g + 1) * M_D_STATE)])
                               for g in range(M_GROUPS)])
    in_cols = np.concatenate([np.concatenate([np.arange(g * M_GW, (g + 1) * M_GW),
                                              d_inner + xbc_cols[g * (M_GW + 2 * M_D_STATE):(g + 1) * (M_GW + 2 * M_D_STATE)]])
                              for g in range(M_GROUPS)])
    n_zx = d_inner + conv_dim
    m_w = jnp.concatenate([m_in_proj[:, :, in_cols], _pad_last(m_in_proj[:, :, n_zx:], M_DT_PAD)], axis=2).astype(BF16)
    m_wo = m_out_proj.astype(BF16)
    m_cw = m_conv_w[:, :, xbc_cols]
    m_cb = _row(m_conv_b[:, xbc_cols])
    m_dtb = _row(_pad_last(m_dt_bias, M_DT_PAD))
    m_al = _row(_pad_last(m_A_log, M_DT_PAD))
    m_dsk = _row(jnp.repeat(m_D, M_HEADDIM, axis=1))
    m_nw = _row(m_norm_w)

    lp = R_LORA_PAD
    blk = np.arange(R_QUAD) // R_HEAD
    rp = {
        "mix": _pad_axis(r_mix, 1, 8),
        "w_rkv": r_w_rkv.astype(BF16), "w_o": r_w_o.astype(BF16),
        "w1": _pad_last(r_w1, lp).astype(BF16), "w2": _pad_axis(r_w2, 1, lp).astype(BF16),
        "a1": _pad_last(r_a1, lp).astype(BF16), "a2": _pad_axis(r_a2, 1, lp).astype(BF16),
        "g1": _pad_last(r_g1, lp).astype(BF16), "g2": _pad_axis(r_g2, 1, lp).astype(BF16),
        "w0": _row(r_w0), "a0": _row(r_a0), "k_k": _row(r_k_k), "k_a": _row(r_k_a),
        "r_k": _row(r_r_k.reshape(r_r_k.shape[0], -1)), "lnx_w": _row(r_lnx_w), "lnx_b": _row(r_lnx_b),
        "bd": jnp.asarray((blk[:, None] == blk[None, :]).astype(np.float32), dtype=BF16),
    }
    if r_v0.shape[0] > 0:
        rp.update({"v0": _row(r_v0), "v1": _pad_last(r_v1, lp).astype(BF16),
                   "v2": _pad_axis(r_v2, 1, lp).astype(BF16)})

    h = x.reshape(t, d)
    v_first = None
    for i in range(depth):
        j = i // 2
        h = _ffn(h, norms3, w_in, w_out, i, 0, tm_ffn)
        if i % 2 == 0:
            h = _mamba_layer(h, norms3, m_w, m_wo, m_cw, m_cb, m_dtb, m_al, m_dsk, m_nw, i, j,
                             batch, seq, lt_mamba)
        else:
            emit_v = j == 0 and depth // 2 > 1
            h, v_layer = _rwkv_layer(h, norms3, i, j, rp, v_first if j > 0 else None, emit_v,
                                     batch, seq, lt_rwkv)
            if j == 0:
                v_first = v_layer
        h = _ffn(h, norms3, w_in, w_out, i, 1, tm_ffn)
    return h.reshape(batch, seq, d)
```

```python
import functools

import numpy as np
import jax
import jax.numpy as jnp
from jax import lax
from jax.experimental import pallas as pl
from jax.experimental.pallas import tpu as pltpu

F32 = jnp.float32
BF16 = jnp.bfloat16

NORM_EPS = 1e-6
N_NORMS = 6
FFN_HALF = 0.5

M_HEADDIM = 64
M_GROUPS = 8
M_HPG = 4
M_D_STATE = 128
M_D_CONV = 4
M_NORM_EPS = 1e-5
M_GW = M_HPG * M_HEADDIM
M_DT_PAD = 128
SSD_Q = 128

R_HEAD = 64
R_LNX_EPS = 64e-5
R_L2_EPS = 1e-12
R_LORA_PAD = 128
R_CHUNK = 64
R_QUAD = 4 * R_HEAD

VMEM_LIMIT = 56 * 1024 * 1024


def _rmsnorm(x, g, eps=NORM_EPS):
    return x * lax.rsqrt(jnp.mean(x * x, axis=-1, keepdims=True) + eps) * g


def _sigmoid(x):
    return 0.5 + 0.5 * jnp.tanh(0.5 * x)


def _silu(x):
    half = 0.5 * x
    return half + half * jnp.tanh(half)


def _softplus(x):
    return jnp.maximum(x, 0.0) + jnp.log(1.0 + jnp.exp(-jnp.abs(x)))


def _dot(a, b):
    return jnp.dot(a, b, preferred_element_type=F32)


def _dot_nt(a, b):
    return lax.dot_general(a, b, (((1,), (1,)), ((), ())), preferred_element_type=F32)


def _dot_tn(a, b):
    return lax.dot_general(a, b, (((0,), (0,)), ((), ())), preferred_element_type=F32)


def _split(x):
    hi = x.astype(BF16)
    lo = (x - hi.astype(F32)).astype(BF16)
    return hi, lo


def _dot_split(hl, m):
    return _dot(hl[0], m) + _dot(hl[1], m)


def _dot_x_exact(x, m):
    return _dot_split(_split(x), m)


def _dot_exact_x(m, x):
    hi, lo = _split(x)
    return _dot(m, hi) + _dot(m, lo)


def _params(sem):
    return pltpu.CompilerParams(dimension_semantics=sem, vmem_limit_bytes=VMEM_LIMIT)


def _resident(block, index_map):
    return pl.BlockSpec(block, index_map, pipeline_mode=pl.Buffered(1))


def _ffn_kernel(x_ref, gpre_ref, gpost_ref, win_ref, wout_ref, o_ref, acc_ref, *, d_ff, cf, sub):
    for s in range(x_ref.shape[0] // sub):
        rows = slice(s * sub, (s + 1) * sub)
        x = x_ref[rows, :]
        xn = _rmsnorm(x, gpre_ref[...]).astype(BF16)
        for c in range(d_ff // cf):
            gate = _dot(xn, win_ref[:, c * cf:(c + 1) * cf])
            up = _dot(xn, win_ref[:, d_ff + c * cf:d_ff + (c + 1) * cf])
            act = (_silu(gate) * up).astype(BF16)
            part = _dot(act, wout_ref[c * cf:(c + 1) * cf, :])
            if c == 0:
                acc_ref[rows, :] = part
            else:
                acc_ref[rows, :] += part
        o_ref[rows, :] = x + FFN_HALF * _rmsnorm(acc_ref[rows, :], gpost_ref[...])


def _ffn(h, norms3, w_in, w_out, layer, which, tm):
    t, d = h.shape
    d_ff = w_out.shape[2]
    n0 = layer * N_NORMS + (0 if which == 0 else 4)
    kern = functools.partial(_ffn_kernel, d_ff=d_ff, cf=256, sub=min(512, tm))
    return pl.pallas_call(
        kern,
        out_shape=jax.ShapeDtypeStruct((t, d), F32),
        grid=(t // tm,),
        in_specs=[
            pl.BlockSpec((tm, d), lambda i: (i, 0)),
            pl.BlockSpec((None, 1, d), lambda i: (n0, 0, 0)),
            pl.BlockSpec((None, 1, d), lambda i: (n0 + 1, 0, 0)),
            _resident((None, None, d, 2 * d_ff), lambda i: (layer, which, 0, 0)),
            _resident((None, None, d_ff, d), lambda i: (layer, which, 0, 0)),
        ],
        out_specs=pl.BlockSpec((tm, d), lambda i: (i, 0)),
        scratch_shapes=[pltpu.VMEM((tm, d), F32)],
        compiler_params=_params(("parallel",)),
        name="ffn",
    )(h, norms3, norms3, w_in, w_out)


def _mamba_layer_kernel(h_ref, gpre_ref, gpost_ref, win_ref, wo_ref, cw_ref, cb_ref, dtb_ref, alog_ref,
                        dskip_ref, nw_ref, o_ref, xpad_ref, z_ref, yn_ref, st_ref, *, lt, q):
    c = pl.program_id(1)
    gcols = M_GW + 2 * M_D_STATE
    n_sub = lt // q

    @pl.when(c == 0)
    def _():
        xpad_ref[0:8, :] = jnp.zeros((8, xpad_ref.shape[1]), F32)
        st_ref[...] = jnp.zeros_like(st_ref)

    @pl.when(c > 0)
    def _():
        xpad_ref[0:8, :] = xpad_ref[lt:lt + 8, :]

    h = h_ref[...]
    u = _rmsnorm(h, gpre_ref[...]).astype(BF16)
    n_grp = M_GROUPS * (M_GW + gcols)
    dt_all = _dot(u, win_ref[:, n_grp:])

    row = lax.broadcasted_iota(jnp.int32, (q, q), 0)
    col = lax.broadcasted_iota(jnp.int32, (q, q), 1)
    causal = row >= col
    tril = jnp.where(causal, 1.0, 0.0).astype(BF16)
    neg_a = -jnp.exp(alog_ref[...])
    pre = []
    for s in range(n_sub):
        dtv = _softplus(dt_all[s * q:(s + 1) * q] + dtb_ref[...])
        a_cum = _dot_exact_x(tril, dtv * neg_a)
        ea = jnp.exp(a_cum)
        to_end = jnp.exp(a_cum[q - 1:q, :] - a_cum) * dtv
        pre.append((a_cum, a_cum.T, dtv.astype(BF16), ea.astype(BF16), to_end.astype(BF16), ea[q - 8:q]))

    erow = lax.broadcasted_iota(jnp.int32, (M_DT_PAD, M_GW), 0)
    ehead = lax.broadcasted_iota(jnp.int32, (M_DT_PAD, M_GW), 1) // M_HEADDIM
    lane_head = lax.broadcasted_iota(jnp.int32, (q, M_GW), 1) // M_HEADDIM

    def project_z(g):
        w0 = g * (M_GW + gcols)
        z_ref[:, g * M_GW:(g + 1) * M_GW] = _dot(u, win_ref[:, w0:w0 + M_GW])

    def project_xbc(g, half):
        w0 = g * (M_GW + gcols) + M_GW + half * M_GW
        c0 = g * gcols + half * M_GW
        xpad_ref[8:8 + lt, c0:c0 + M_GW] = _dot(u, win_ref[:, w0:w0 + M_GW])

    out_acc = []

    def out_partial(g):
        part = _dot(yn_ref[:, g * M_GW:(g + 1) * M_GW], wo_ref[g * M_GW:(g + 1) * M_GW, :])
        out_acc[:] = [part if not out_acc else out_acc[0] + part]

    def ssd(s, g, fillers):
        fillers = list(fillers)
        fill = lambda: fillers.pop(0)() if fillers else None
        a_cum, a_cum_t, dtv_bf, ea_bf, te_bf, ea_tail = pre[s]
        r0 = s * q
        cols = slice(g * gcols, (g + 1) * gcols)
        fill()
        xa = xpad_ref[r0:r0 + q + 8, cols]
        acc = cb_ref[:, cols] + cw_ref[M_D_CONV - 1:M_D_CONV, cols] * xa[8:]
        for sh in range(1, M_D_CONV):
            k = M_D_CONV - 1 - sh
            acc = acc + cw_ref[k:k + 1, cols] * pltpu.roll(xa, sh, 0)[8:]
        xbc = _silu(acc)
        xg = xbc[:, 0:M_GW]
        bg = xbc[:, M_GW:M_GW + M_D_STATE]
        cg_bf = xbc[:, M_GW + M_D_STATE:].astype(BF16)

        expand = jnp.where(erow == M_HPG * g + ehead, 1.0, 0.0).astype(BF16)
        dt_x = _dot(dtv_bf, expand)
        ea_x = _dot(ea_bf, expand)
        te_x = _dot(te_bf, expand)
        ea_end = _dot_x_exact(ea_tail, expand)[7:8, :]
        fill()

        cbm = _dot_nt(cg_bf, bg.astype(BF16))
        fill()
        xdt = (xg * dt_x).astype(BF16)
        w_heads, x_heads = [], []
        for e in range(M_HPG):
            hd = M_HPG * g + e
            seg = a_cum[:, hd:hd + 1] - a_cum_t[hd:hd + 1, :]
            dec = jnp.where(causal, jnp.exp(jnp.minimum(seg, 0.0)), 0.0)
            w_heads.append((cbm * dec).astype(BF16))
            x_heads.append(jnp.where(lane_head == e, xdt, jnp.zeros_like(xdt)))
        y = _dot(jnp.concatenate(w_heads, axis=1), jnp.concatenate(x_heads, axis=0))
        fill()

        st = st_ref[g]
        y = y + _dot(cg_bf, st.astype(BF16)) * ea_x
        xte = (xg * te_x).astype(BF16)
        st_ref[g] = st * ea_end + _dot(bg.T.astype(BF16), xte)
        while fillers:
            fill()

        gl = slice(g * M_GW, (g + 1) * M_GW)
        y = y + xg * dskip_ref[:, gl]
        y = y * _silu(z_ref[r0:r0 + q, gl])
        ms = jnp.mean(y * y, axis=-1, keepdims=True)
        yn_ref[r0:r0 + q, gl] = (y * lax.rsqrt(ms + M_NORM_EPS) * nw_ref[:, gl]).astype(BF16)

    project_z(0)
    project_xbc(0, 0)
    project_xbc(0, 1)
    for g in range(M_GROUPS):
        nxt = g + 1
        ahead = [functools.partial(project_xbc, nxt, 0), functools.partial(project_xbc, nxt, 1),
                 functools.partial(project_z, nxt)] if nxt < M_GROUPS else []
        behind = [functools.partial(out_partial, g - 1)] if g > 0 else []
        for s in range(n_sub):
            last = s == n_sub - 1
            ssd(s, g, (ahead if s == 0 else []) + (behind if last else []))
    out_partial(M_GROUPS - 1)
    o_ref[...] = h + _rmsnorm(out_acc[0], gpost_ref[...])


def _mamba_layer(h, norms3, w_in, w_out, conv_w, conv_b, dt_bias, a_log, d_skip, norm_w, layer, j,
                 batch, seq, lt):
    t, d = h.shape
    d_inner = w_out.shape[1]
    conv_dim = conv_w.shape[2]
    q = min(SSD_Q, lt)
    nt = seq // lt
    tok = lambda b, c: (b * nt + c, 0)
    par = lambda b, c: (j, 0, 0)
    kern = functools.partial(_mamba_layer_kernel, lt=lt, q=q)
    return pl.pallas_call(
        kern,
        out_shape=jax.ShapeDtypeStruct((t, d), F32),
        grid=(batch, nt),
        in_specs=[
            pl.BlockSpec((lt, d), tok),
            pl.BlockSpec((None, 1, d), lambda b, c: (layer * N_NORMS + 2, 0, 0)),
            pl.BlockSpec((None, 1, d), lambda b, c: (layer * N_NORMS + 3, 0, 0)),
            _resident((None, d, w_in.shape[2]), par),
            _resident((None, d_inner, d), par),
            pl.BlockSpec((None, M_D_CONV, conv_dim), par),
            pl.BlockSpec((None, 1, conv_dim), par),
            pl.BlockSpec((None, 1, M_DT_PAD), par),
            pl.BlockSpec((None, 1, M_DT_PAD), par),
            pl.BlockSpec((None, 1, d_inner), par),
            pl.BlockSpec((None, 1, d_inner), par),
        ],
        out_specs=pl.BlockSpec((lt, d), tok),
        scratch_shapes=[pltpu.VMEM((lt + 8, conv_dim), F32),
                        pltpu.VMEM((lt, d_inner), F32),
                        pltpu.VMEM((lt, d_inner), BF16),
                        pltpu.VMEM((M_GROUPS, M_D_STATE, M_GW), F32)],
        compiler_params=_params(("parallel", "arbitrary")),
        name="mamba_layer",
    )(h, norms3, norms3, w_in, w_out, conv_w, conv_b, dt_bias, a_log, d_skip, norm_w)


def _head_sum(x, bd_ref):
    d = x.shape[1]
    cols = [_dot(x[:, c * R_QUAD:(c + 1) * R_QUAD].astype(BF16), bd_ref[...]) for c in range(d // R_QUAD)]
    return jnp.concatenate(cols, axis=1)


def _rwkv_project(h_ref, hp_ref, g_ref, mix_ref, wrkv_ref, w1_ref, w2_ref, a1_ref, a2_ref, g1_ref, g2_ref,
                  w0_ref, a0_ref, kk_ref, ka_ref, bd_ref, vres, seq_start,
                  r_ref, lw_ref, k_ref, v_ref, a_ref, b_ref, gate_ref):
    gn = g_ref[...]
    u = _rmsnorm(h_ref[...], gn)
    u_before = _rmsnorm(hp_ref[7:8, :], gn)
    u_before = jnp.where(seq_start, jnp.zeros_like(u_before), u_before)
    row = lax.broadcasted_iota(jnp.int32, u.shape, 0)
    u_prev = jnp.where(row == 0, u_before, pltpu.roll(u, 1, 0))
    u_bf = u.astype(BF16)
    delta_bf = (u_prev - u).astype(BF16)

    def shift(i):
        return u_bf + delta_bf * mix_ref[i:i + 1, :].astype(BF16)

    r_ref[...] = _dot(shift(0), wrkv_ref[0])
    k = _dot(shift(2), wrkv_ref[1])
    xv = shift(3)
    v = _dot(xv, wrkv_ref[2])
    if vres is not None:
        v0_ref, v1_ref, v2_ref, vf_ref = vres
        mixv = _sigmoid(v0_ref[...] + _dot(_dot(xv, v1_ref[...]).astype(BF16), v2_ref[...]))
        v = v + (vf_ref[...] - v) * mixv
    v_ref[...] = v
    wl = w0_ref[...] + _dot(jnp.tanh(_dot(shift(1), w1_ref[...])).astype(BF16), w2_ref[...])
    lw_ref[...] = -jnp.exp(-_softplus(-wl) - 0.5)
    gate_ref[...] = _dot(_sigmoid(_dot(shift(5), g1_ref[...])).astype(BF16), g2_ref[...])
    alpha = _sigmoid(a0_ref[...] + _dot(_dot(shift(4), a1_ref[...]).astype(BF16), a2_ref[...]))
    kk = k * kk_ref[...]
    kk = kk / jnp.maximum(jnp.sqrt(_head_sum(kk * kk, bd_ref)), R_L2_EPS)
    k_ref[...] = k * (1.0 + (alpha - 1.0) * ka_ref[...])
    a_ref[...] = -kk
    b_ref[...] = kk * alpha


def _rec_masks():
    li = np.arange(R_CHUNK)[:, None]
    lj = np.arange(R_CHUNK)[None, :]
    masks = [lj < li, lj <= li, (lj < li) & (li // 8 == lj // 8)]
    for m in (8, 16, 32):
        masks.append((li // (2 * m) == lj // (2 * m)) & (li % (2 * m) >= m) & (lj % (2 * m) < m))
    return jnp.asarray(np.stack([np.tile(m, (1, R_QUAD // R_HEAD)) for m in masks]).astype(np.float32))


def _rwkv_recurrence(r_ref, lw_ref, k_ref, v_ref, a_ref, b_ref, masks_ref, bd_ref, y_ref,
                     st_ref, qp_ref, g_ref, ct_ref, pe_ref, *, n_chunks, unroll):
    lc = R_CHUNK
    n_quads = r_ref.shape[1] // R_QUAD
    ti = lax.broadcasted_iota(jnp.int32, (lc, lc), 0)
    tj = lax.broadcasted_iota(jnp.int32, (lc, lc), 1)
    tril = jnp.where(ti >= tj, 1.0, 0.0).astype(BF16)
    lane_head = lax.broadcasted_iota(jnp.int32, (lc, R_QUAD), 1) // R_HEAD

    def quad(x):
        x = x.astype(BF16)
        zero = jnp.zeros_like(x)
        return jnp.concatenate([jnp.where(lane_head == hd, x, zero) for hd in range(R_QUAD // R_HEAD)], axis=0)

    each = lambda f, *ls: [f(*xs) for xs in zip(*ls)]
    hdot = lambda xs, ys: each(lambda x, y: _dot(x.astype(BF16), quad(y)), xs, ys)
    add3 = lambda xs, ys, zs: each(lambda x, y, z: x + y + z, xs, ys, zs)

    def tri_inverse_offdiag(n_mats):
        d1 = [n * masks_ref[2] for n in n_mats]
        d2 = hdot(d1, d1)
        d4 = hdot(d2, d2)
        xa = add3(d1, d2, hdot(d1, d2))
        x = add3(xa, d4, hdot(xa, d4))
        for lvl in range(3):
            nm = [n * masks_ref[3 + lvl] for n in n_mats]
            w = each(lambda p, q: p + q, nm, hdot(nm, x))
            x = add3(x, w, hdot(x, w))
        return x

    def build(insts):
        sls = [(pl.ds(pl.multiple_of(ci * lc, lc), lc), slice(q * R_QUAD, (q + 1) * R_QUAD)) for ci, q in insts]
        lw = [lw_ref[sl] for sl in sls]
        logp = [_dot_exact_x(tril, x) for x in lw]
        logp_end = [x[lc - 1:lc, :] for x in logp]
        pinv = [jnp.exp(-x) for x in logp]
        to_end = each(lambda e, x: jnp.exp(e - x), logp_end, logp)
        a_hat = [a_ref[sl] * jnp.exp(x - w) for sl, x, w in zip(sls, logp, lw)]
        q_un = [r_ref[sl] * jnp.exp(x) for sl, x in zip(sls, logp)]
        kc = [k_ref[sl] for sl in sls]
        bc = [b_ref[sl] for sl in sls]
        vc = [v_ref[sl] for sl in sls]
        lhs = each(lambda a, q: jnp.concatenate([a, q], axis=0).astype(BF16), a_hat, q_un)
        rhs = each(lambda b, k, pi: jnp.concatenate([quad(b * pi), quad(k * pi)], axis=0), bc, kc, pinv)
        a4 = each(_dot_nt, lhs, rhs)
        strict = masks_ref[0]
        incl = masks_ref[1]
        n_mat = [x[0:lc, 0:R_QUAD] * strict for x in a4]
        a_ak = [x[0:lc, R_QUAD:] * strict for x in a4]
        a_qb = [x[lc:, 0:R_QUAD] * incl for x in a4]
        a_qk = [(x[lc:, R_QUAD:] * incl).astype(BF16) for x in a4]

        v_q = [quad(x) for x in vc]
        z = each(lambda a, v: _dot(a.astype(BF16), v), a_ak, v_q)
        x = tri_inverse_offdiag(n_mat)
        xaz = each(lambda xx, a, zz: _dot(xx.astype(BF16), jnp.concatenate([quad(a), quad(zz)], axis=1)),
                   x, a_hat, z)
        ap = each(lambda a, d: a + d[:, 0:R_QUAD], a_hat, xaz)
        u0 = each(lambda zz, d: zz + d[:, R_QUAD:], z, xaz)
        qy = each(lambda m, a, u: _dot(m.astype(BF16), jnp.concatenate([quad(a), quad(u)], axis=1)),
                  a_qb, ap, u0)
        y0 = each(_dot, a_qk, v_q)
        bt = each(lambda b, e: (b * e).astype(BF16), bc, to_end)
        kt = each(lambda k, e: (k * e).astype(BF16), kc, to_end)
        g = each(lambda a, b: _dot_tn(a.astype(BF16), b), ap, bt)
        ct = each(lambda u, v, b, k: _dot_tn(jnp.concatenate([u.astype(BF16), v.astype(BF16)], axis=0),
                                             jnp.concatenate([b, k], axis=0)), u0, vc, bt, kt)
        same_head = bd_ref[...] > 0
        for i, (ci, q) in enumerate(insts):
            qp_ref[sls[i]] = (q_un[i] + qy[i][:, 0:R_QUAD]).astype(BF16)
            y_ref[sls[i]] = qy[i][:, R_QUAD:] + y0[i]
            g_ref[ci, q] = jnp.where(same_head, g[i], 0.0).astype(BF16)
            ct_ref[ci, q] = jnp.where(same_head, ct[i], 0.0)
            pe_ref[ci, :, q * R_QUAD:(q + 1) * R_QUAD] = jnp.broadcast_to(jnp.exp(logp_end[i]), (8, R_QUAD))

    def phase_a(i, carry):
        build([(i * unroll + u, q) for u in range(unroll) for q in range(n_quads)])
        return carry

    lax.fori_loop(0, n_chunks // unroll, phase_a, 0)

    def phase_b(ci, carry):
        rows = pl.ds(pl.multiple_of(ci * lc, lc), lc)
        lanes = [slice(q * R_QUAD, (q + 1) * R_QUAD) for q in range(n_quads)]
        st = [st_ref[q] for q in range(n_quads)]
        st_bf = [s.astype(BF16) for s in st]
        sg = [_dot(st_bf[q], g_ref[ci, q]) for q in range(n_quads)]
        yq = [_dot_nt(qp_ref[rows, lanes[q]], st_bf[q]) for q in range(n_quads)]
        for q in range(n_quads):
            st_ref[q] = st[q] * pe_ref[ci, 0:1, lanes[q]] + sg[q] + ct_ref[ci, q]
            y_ref[rows, lanes[q]] += yq[q]
        return carry

    lax.fori_loop(0, n_chunks, phase_b, 0)


def _rwkv_finish(y_ref, r_ref, k_ref, v_ref, gate_ref, lw_ref, lb_ref, rk_ref, bd_ref, wo_ref,
                 h_ref, g_ref, o_ref):
    y = y_ref[...]
    inv_n = 1.0 / R_HEAD
    mu = _head_sum(y, bd_ref) * inv_n
    yc = y - mu
    var = _head_sum(yc * yc, bd_ref) * inv_n
    yn = yc * lax.rsqrt(var + R_LNX_EPS) * lw_ref[...] + lb_ref[...]
    yn = yn + _head_sum(r_ref[...] * k_ref[...] * rk_ref[...], bd_ref) * v_ref[...]
    t = _dot((yn * gate_ref[...]).astype(BF16), wo_ref[...])
    o_ref[...] = h_ref[...] + _rmsnorm(t, g_ref[...])


def _rwkv_layer_kernel(*refs, has_vres, emit_v, n_chunks, unroll):
    (h_ref, hp_ref, gpre_ref, gpost_ref, mix_ref, wrkv_ref, w1_ref, w2_ref, a1_ref, a2_ref, g1_ref, g2_ref,
     w0_ref, a0_ref, kk_ref, ka_ref, bd_ref, lnw_ref, lnb_ref, rk_ref, wo_ref, masks_ref) = refs[:22]
    pos = 22
    vres = None
    if has_vres:
        vres = refs[pos:pos + 4]
        pos += 4
    o_ref = refs[pos]
    pos += 1
    if emit_v:
        v_ref = refs[pos]
        pos += 1
    (r_s, lw_s, k_s, a_s, b_s, gate_s, y_s) = refs[pos:pos + 7]
    pos += 7
    if not emit_v:
        v_ref = refs[pos]
        pos += 1
    st_s, qp_s, g_s, ct_s, pe_s = refs[pos:pos + 5]

    seq_start = pl.program_id(1) == 0

    @pl.when(seq_start)
    def _():
        st_s[...] = jnp.zeros_like(st_s)

    _rwkv_project(h_ref, hp_ref, gpre_ref, mix_ref, wrkv_ref, w1_ref, w2_ref, a1_ref, a2_ref, g1_ref, g2_ref,
                  w0_ref, a0_ref, kk_ref, ka_ref, bd_ref, vres, seq_start,
                  r_s, lw_s, k_s, v_ref, a_s, b_s, gate_s)
    _rwkv_recurrence(r_s, lw_s, k_s, v_ref, a_s, b_s, masks_ref, bd_ref, y_s,
                     st_s, qp_s, g_s, ct_s, pe_s, n_chunks=n_chunks, unroll=unroll)
    _rwkv_finish(y_s, r_s, k_s, v_ref, gate_s, lnw_ref, lnb_ref, rk_ref, bd_ref, wo_ref, h_ref, gpost_ref, o_ref)


def _rwkv_layer(h, norms3, layer, j, p, v_first, emit_v, batch, seq, lt):
    t, d = h.shape
    nt = seq // lt
    nch = lt // R_CHUNK
    nq = d // R_QUAD
    tok = lambda b, i: (b * nt + i, 0)
    prev = lambda b, i: (jnp.maximum((b * seq + i * lt) // 8 - 1, 0), 0)
    par3 = lambda b, i: (j, 0, 0)
    row = lambda: pl.BlockSpec((None, 1, d), par3)
    lp = R_LORA_PAD
    has_vres = v_first is not None
    masks = _rec_masks()
    in_specs = [
        pl.BlockSpec((lt, d), tok),
        pl.BlockSpec((8, d), prev),
        pl.BlockSpec((None, 1, d), lambda b, i: (layer * N_NORMS + 2, 0, 0)),
        pl.BlockSpec((None, 1, d), lambda b, i: (layer * N_NORMS + 3, 0, 0)),
        pl.BlockSpec((None, 8, d), par3),
        _resident((None, 3, d, d), lambda b, i: (j, 0, 0, 0)),
        _resident((None, d, lp), par3), _resident((None, lp, d), par3),
        _resident((None, d, lp), par3), _resident((None, lp, d), par3),
        _resident((None, d, lp), par3), _resident((None, lp, d), par3),
        row(), row(), row(), row(),
        _resident((R_QUAD, R_QUAD), lambda b, i: (0, 0)),
        row(), row(), row(),
        _resident((None, d, d), par3),
        _resident(masks.shape, lambda b, i: (0, 0, 0)),
    ]
    args = [h, h, norms3, norms3, p["mix"], p["w_rkv"], p["w1"], p["w2"], p["a1"], p["a2"], p["g1"], p["g2"],
            p["w0"], p["a0"], p["k_k"], p["k_a"], p["bd"], p["lnx_w"], p["lnx_b"], p["r_k"], p["w_o"], masks]
    if has_vres:
        parv = lambda b, i: (j - 1, 0, 0)
        in_specs += [pl.BlockSpec((None, 1, d), parv), _resident((None, d, lp), parv),
                     _resident((None, lp, d), parv), pl.BlockSpec((lt, d), tok)]
        args += [p["v0"], p["v1"], p["v2"], v_first]
    n_out = 2 if emit_v else 1
    scratch = [pltpu.VMEM((lt, d), F32) for _ in range(7 if emit_v else 8)]
    scratch += [pltpu.VMEM((nq, R_QUAD, R_QUAD), F32),
                pltpu.VMEM((lt, d), BF16),
                pltpu.VMEM((nch, nq, R_QUAD, R_QUAD), BF16),
                pltpu.VMEM((nch, nq, R_QUAD, R_QUAD), F32),
                pltpu.VMEM((nch, 8, d), F32)]
    kern = functools.partial(_rwkv_layer_kernel, has_vres=has_vres, emit_v=emit_v, n_chunks=nch,
                             unroll=4 if nch % 4 == 0 else 1)
    outs = pl.pallas_call(
        kern,
        out_shape=tuple(jax.ShapeDtypeStruct((t, d), F32) for _ in range(n_out)),
        grid=(batch, nt),
        in_specs=in_specs,
        out_specs=tuple(pl.BlockSpec((lt, d), tok) for _ in range(n_out)),
        scratch_shapes=scratch,
        compiler_params=_params(("parallel", "arbitrary")),
        name="rwkv_layer",
    )(*args)
    return outs if emit_v else (outs[0], None)


def _pad_last(x, n):
    return jnp.pad(x, [(0, 0)] * (x.ndim - 1) + [(0, n - x.shape[-1])])


def _pad_axis(x, axis, n):
    pads = [(0, 0)] * x.ndim
    pads[axis] = (0, n - x.shape[axis])
    return jnp.pad(x, pads)


def _row(x):
    return x.reshape(x.shape[0], 1, -1)


def kernel(x, norms, ffn_w_in, ffn_w_out, m_in_proj, m_conv_w, m_conv_b, m_dt_bias, m_A_log, m_D, m_norm_w, m_out_proj, r_mix, r_w_rkv, r_w_o, r_w0, r_w1, r_w2, r_a0, r_a1, r_a2, r_v0, r_v1, r_v2, r_g1, r_g2, r_k_k, r_k_a, r_r_k, r_lnx_w, r_lnx_b):
    batch, seq, d = x.shape
    depth = norms.shape[0]
    t = batch * seq
    d_inner = m_out_proj.shape[1]
    conv_dim = m_conv_w.shape[2]
    m_heads = m_dt_bias.shape[1]
    assert m_heads <= M_DT_PAD and d_inner == M_GROUPS * M_GW and d % R_QUAD == 0
    assert conv_dim == d_inner + 2 * M_GROUPS * M_D_STATE

    tm_ffn = min(1024, t)
    lt_mamba = min(256, seq)
    lt_rwkv = min(256, seq)

    norms3 = _row(norms.reshape(depth * N_NORMS, d))
    w_in = ffn_w_in.astype(BF16)
    w_out = ffn_w_out.astype(BF16)
    gn = M_GROUPS * M_D_STATE

    def group_major(w, parts):
        lead, off, cols = w.shape[:-1], 0, []
        for width in parts:
            cols.append(w[..., off:off + width].reshape(lead + (M_GROUPS, width // M_GROUPS)))
            off += width
        return jnp.concatenate(cols, axis=-1).reshape(lead + (off,))

    n_zx = d_inner + conv_dim
    m_in_bf = m_in_proj.astype(BF16)
    m_w = jnp.concatenate([group_major(m_in_bf[:, :, :n_zx], (d_inner, d_inner, gn, gn)),
                           _pad_last(m_in_bf[:, :, n_zx:], M_DT_PAD)], axis=2)
    m_wo = m_out_proj.astype(BF16)
    m_cw = group_major(m_conv_w, (d_inner, gn, gn))
    m_cb = _row(group_major(m_conv_b, (d_inner, gn, gn)))
    m_dtb = _row(_pad_last(m_dt_bias, M_DT_PAD))
    m_al = _row(_pad_last(m_A_log, M_DT_PAD))
    m_dsk = _row(jnp.repeat(m_D, M_HEADDIM, axis=1))
    m_nw = _row(m_norm_w)

    lp = R_LORA_PAD
    blk = np.arange(R_QUAD) // R_HEAD
    rp = {
        "mix": _pad_axis(r_mix, 1, 8),
        "w_rkv": r_w_rkv.astype(BF16), "w_o": r_w_o.astype(BF16),
        "w1": _pad_last(r_w1, lp).astype(BF16), "w2": _pad_axis(r_w2, 1, lp).astype(BF16),
        "a1": _pad_last(r_a1, lp).astype(BF16), "a2": _pad_axis(r_a2, 1, lp).astype(BF16),
        "g1": _pad_last(r_g1, lp).astype(BF16), "g2": _pad_axis(r_g2, 1, lp).astype(BF16),
        "w0": _row(r_w0), "a0": _row(r_a0), "k_k": _row(r_k_k), "k_a": _row(r_k_a),
        "r_k": _row(r_r_k.reshape(r_r_k.shape[0], -1)), "lnx_w": _row(r_lnx_w), "lnx_b": _row(r_lnx_b),
        "bd": jnp.asarray((blk[:, None] == blk[None, :]).astype(np.float32), dtype=BF16),
    }
    if r_v0.shape[0] > 0:
        rp.update({"v0": _row(r_v0), "v1": _pad_last(r_v1, lp).astype(BF16),
                   "v2": _pad_axis(r_v2, 1, lp).astype(BF16)})

    h = x.reshape(t, d)
    v_first = None
    for i in range(depth):
        j = i // 2
        h = _ffn(h, norms3, w_in, w_out, i, 0, tm_ffn)
        if i % 2 == 0:
            h = _mamba_layer(h, norms3, m_w, m_wo, m_cw, m_cb, m_dtb, m_al, m_dsk, m_nw, i, j,
                             batch, seq, lt_mamba)
        else:
            emit_v = j == 0 and depth // 2 > 1
            h, v_layer = _rwkv_layer(h, norms3, i, j, rp, v_first if j > 0 else None, emit_v,
                                     batch, seq, lt_rwkv)
            if j == 0:
                v_first = v_layer
        h = _ffn(h, norms3, w_in, w_out, i, 1, tm_ffn)
    return h.reshape(batch, seq, d)
```

```python
import functools

import numpy as np
import jax
import jax.numpy as jnp
from jax import lax
from jax.experimental import pallas as pl
from jax.experimental.pallas import tpu as pltpu

F32 = jnp.float32
BF16 = jnp.bfloat16

NORM_EPS = 1e-6
N_NORMS = 6
FFN_HALF = 0.5

M_HEADDIM = 64
M_GROUPS = 8
M_HPG = 4
M_D_STATE = 128
M_D_CONV = 4
M_NORM_EPS = 1e-5
M_GW = M_HPG * M_HEADDIM
M_DT_PAD = 128
SSD_Q = 128

R_HEAD = 64
R_LNX_EPS = 64e-5
R_L2_EPS = 1e-12
R_LORA_PAD = 128
R_CHUNK = 64
R_QUAD = 4 * R_HEAD

VMEM_LIMIT = 56 * 1024 * 1024


def _rmsnorm(x, g, eps=NORM_EPS):
    return x * lax.rsqrt(jnp.mean(x * x, axis=-1, keepdims=True) + eps) * g


def _sigmoid(x):
    return 0.5 + 0.5 * jnp.tanh(0.5 * x)


def _silu(x):
    half = 0.5 * x
    return half + half * jnp.tanh(half)


def _softplus(x):
    return jnp.maximum(x, 0.0) + jnp.log(1.0 + jnp.exp(-jnp.abs(x)))


def _dot(a, b):
    return jnp.dot(a, b, preferred_element_type=F32)


def _dot_nt(a, b):
    return lax.dot_general(a, b, (((1,), (1,)), ((), ())), preferred_element_type=F32)


def _dot_tn(a, b):
    return lax.dot_general(a, b, (((0,), (0,)), ((), ())), preferred_element_type=F32)


def _split(x):
    hi = x.astype(BF16)
    lo = (x - hi.astype(F32)).astype(BF16)
    return hi, lo


def _dot_split(hl, m):
    return _dot(hl[0], m) + _dot(hl[1], m)


def _dot_x_exact(x, m):
    return _dot_split(_split(x), m)


def _dot_exact_x(m, x):
    hi, lo = _split(x)
    return _dot(m, hi) + _dot(m, lo)


def _params(sem):
    return pltpu.CompilerParams(dimension_semantics=sem, vmem_limit_bytes=VMEM_LIMIT)


def _resident(block, index_map):
    return pl.BlockSpec(block, index_map, pipeline_mode=pl.Buffered(1))


def _ffn_kernel(x_ref, gpre_ref, gpost_ref, win_ref, wout_ref, o_ref, acc_ref, *, d_ff, cf, sub):
    for s in range(x_ref.shape[0] // sub):
        rows = slice(s * sub, (s + 1) * sub)
        x = x_ref[rows, :]
        xn = _rmsnorm(x, gpre_ref[...]).astype(BF16)
        for c in range(d_ff // cf):
            gate = _dot(xn, win_ref[:, c * cf:(c + 1) * cf])
            up = _dot(xn, win_ref[:, d_ff + c * cf:d_ff + (c + 1) * cf])
            act = (_silu(gate) * up).astype(BF16)
            part = _dot(act, wout_ref[c * cf:(c + 1) * cf, :])
            if c == 0:
                acc_ref[rows, :] = part
            else:
                acc_ref[rows, :] += part
        o_ref[rows, :] = x + FFN_HALF * _rmsnorm(acc_ref[rows, :], gpost_ref[...])


def _ffn(h, norms3, w_in, w_out, layer, which, tm):
    t, d = h.shape
    d_ff = w_out.shape[2]
    n0 = layer * N_NORMS + (0 if which == 0 else 4)
    kern = functools.partial(_ffn_kernel, d_ff=d_ff, cf=256, sub=min(512, tm))
    return pl.pallas_call(
        kern,
        out_shape=jax.ShapeDtypeStruct((t, d), F32),
        grid=(t // tm,),
        in_specs=[
            pl.BlockSpec((tm, d), lambda i: (i, 0)),
            pl.BlockSpec((None, 1, d), lambda i: (n0, 0, 0)),
            pl.BlockSpec((None, 1, d), lambda i: (n0 + 1, 0, 0)),
            _resident((None, None, d, 2 * d_ff), lambda i: (layer, which, 0, 0)),
            _resident((None, None, d_ff, d), lambda i: (layer, which, 0, 0)),
        ],
        out_specs=pl.BlockSpec((tm, d), lambda i: (i, 0)),
        scratch_shapes=[pltpu.VMEM((tm, d), F32)],
        compiler_params=_params(("parallel",)),
        name="ffn",
    )(h, norms3, norms3, w_in, w_out)


def _mamba_layer_kernel(h_ref, gpre_ref, gpost_ref, win_ref, wo_ref, cw_ref, cb_ref, dtb_ref, alog_ref,
                        dskip_ref, nw_ref, o_ref, xpad_ref, z_ref, yn_ref, st_ref, *, lt, q):
    c = pl.program_id(1)
    gcols = M_GW + 2 * M_D_STATE
    n_sub = lt // q

    @pl.when(c == 0)
    def _():
        xpad_ref[0:8, :] = jnp.zeros((8, xpad_ref.shape[1]), F32)
        st_ref[...] = jnp.zeros_like(st_ref)

    @pl.when(c > 0)
    def _():
        xpad_ref[0:8, :] = xpad_ref[lt:lt + 8, :]

    h = h_ref[...]
    u = _rmsnorm(h, gpre_ref[...]).astype(BF16)
    n_grp = M_GROUPS * (M_GW + gcols)
    dt_all = _dot(u, win_ref[:, n_grp:])

    row = lax.broadcasted_iota(jnp.int32, (q, q), 0)
    col = lax.broadcasted_iota(jnp.int32, (q, q), 1)
    causal = row >= col
    tril = jnp.where(causal, 1.0, 0.0).astype(BF16)
    neg_a = -jnp.exp(alog_ref[...])
    pre = []
    for s in range(n_sub):
        dtv = _softplus(dt_all[s * q:(s + 1) * q] + dtb_ref[...])
        a_cum = _dot_exact_x(tril, dtv * neg_a)
        ea = jnp.exp(a_cum)
        to_end = jnp.exp(a_cum[q - 1:q, :] - a_cum) * dtv
        pre.append((a_cum, a_cum.T, dtv.astype(BF16), ea.astype(BF16), to_end.astype(BF16), ea[q - 8:q]))

    erow = lax.broadcasted_iota(jnp.int32, (M_DT_PAD, M_GW), 0)
    ehead = lax.broadcasted_iota(jnp.int32, (M_DT_PAD, M_GW), 1) // M_HEADDIM
    lane_head = lax.broadcasted_iota(jnp.int32, (q, M_GW), 1) // M_HEADDIM

    def project_z(g):
        w0 = g * (M_GW + gcols)
        z_ref[:, g * M_GW:(g + 1) * M_GW] = _dot(u, win_ref[:, w0:w0 + M_GW])

    def project_xbc(g, half):
        w0 = g * (M_GW + gcols) + M_GW + half * M_GW
        c0 = g * gcols + half * M_GW
        xpad_ref[8:8 + lt, c0:c0 + M_GW] = _dot(u, win_ref[:, w0:w0 + M_GW])

    out_acc = []

    def out_partial(g):
        part = _dot(yn_ref[:, g * M_GW:(g + 1) * M_GW], wo_ref[g * M_GW:(g + 1) * M_GW, :])
        out_acc[:] = [part if not out_acc else out_acc[0] + part]

    def ssd(s, g, fillers):
        fillers = list(fillers)
        fill = lambda: fillers.pop(0)() if fillers else None
        a_cum, a_cum_t, dtv_bf, ea_bf, te_bf, ea_tail = pre[s]
        r0 = s * q
        cols = slice(g * gcols, (g + 1) * gcols)
        fill()
        xa = xpad_ref[r0:r0 + q + 8, cols]
        acc = cb_ref[:, cols] + cw_ref[M_D_CONV - 1:M_D_CONV, cols] * xa[8:]
        for sh in range(1, M_D_CONV):
            k = M_D_CONV - 1 - sh
            acc = acc + cw_ref[k:k + 1, cols] * pltpu.roll(xa, sh, 0)[8:]
        xbc = _silu(acc)
        xg = xbc[:, 0:M_GW]
        bg = xbc[:, M_GW:M_GW + M_D_STATE]
        cg_bf = xbc[:, M_GW + M_D_STATE:].astype(BF16)

        expand = jnp.where(erow == M_HPG * g + ehead, 1.0, 0.0).astype(BF16)
        dt_x = _dot(dtv_bf, expand)
        ea_x = _dot(ea_bf, expand)
        te_x = _dot(te_bf, expand)
        ea_end = _dot_x_exact(ea_tail, expand)[7:8, :]
        fill()

        cbm = _dot_nt(cg_bf, bg.astype(BF16))
        fill()
        xdt = (xg * dt_x).astype(BF16)
        w_heads, x_heads = [], []
        for e in range(M_HPG):
            hd = M_HPG * g + e
            seg = a_cum[:, hd:hd + 1] - a_cum_t[hd:hd + 1, :]
            dec = jnp.where(causal, jnp.exp(jnp.minimum(seg, 0.0)), 0.0)
            w_heads.append((cbm * dec).astype(BF16))
            x_heads.append(jnp.where(lane_head == e, xdt, jnp.zeros_like(xdt)))
        y = _dot(jnp.concatenate(w_heads, axis=1), jnp.concatenate(x_heads, axis=0))
        fill()

        st = st_ref[g]
        y = y + _dot(cg_bf, st.astype(BF16)) * ea_x
        xte = (xg * te_x).astype(BF16)
        st_ref[g] = st * ea_end + _dot(bg.T.astype(BF16), xte)
        while fillers:
            fill()

        gl = slice(g * M_GW, (g + 1) * M_GW)
        y = y + xg * dskip_ref[:, gl]
        y = y * _silu(z_ref[r0:r0 + q, gl])
        ms = jnp.mean(y * y, axis=-1, keepdims=True)
        yn_ref[r0:r0 + q, gl] = (y * lax.rsqrt(ms + M_NORM_EPS) * nw_ref[:, gl]).astype(BF16)

    project_z(0)
    project_xbc(0, 0)
    project_xbc(0, 1)
    for g in range(M_GROUPS):
        nxt = g + 1
        ahead = [functools.partial(project_xbc, nxt, 0), functools.partial(project_xbc, nxt, 1),
                 functools.partial(project_z, nxt)] if nxt < M_GROUPS else []
        behind = [functools.partial(out_partial, g - 1)] if g > 0 else []
        n_early = max(n_sub - 1, 1)
        for s in range(n_sub):
            last = s == n_sub - 1
            ssd(s, g, (ahead[s::n_early] if s < n_early else []) + (behind if last else []))
    out_partial(M_GROUPS - 1)
    o_ref[...] = h + _rmsnorm(out_acc[0], gpost_ref[...])


def _mamba_layer(h, norms3, w_in, w_out, conv_w, conv_b, dt_bias, a_log, d_skip, norm_w, layer, j,
                 batch, seq, lt):
    t, d = h.shape
    d_inner = w_out.shape[1]
    conv_dim = conv_w.shape[2]
    q = min(SSD_Q, lt)
    nt = seq // lt
    tok = lambda b, c: (b * nt + c, 0)
    par = lambda b, c: (j, 0, 0)
    kern = functools.partial(_mamba_layer_kernel, lt=lt, q=q)
    return pl.pallas_call(
        kern,
        out_shape=jax.ShapeDtypeStruct((t, d), F32),
        grid=(batch, nt),
        in_specs=[
            pl.BlockSpec((lt, d), tok),
            pl.BlockSpec((None, 1, d), lambda b, c: (layer * N_NORMS + 2, 0, 0)),
            pl.BlockSpec((None, 1, d), lambda b, c: (layer * N_NORMS + 3, 0, 0)),
            _resident((None, d, w_in.shape[2]), par),
            _resident((None, d_inner, d), par),
            pl.BlockSpec((None, M_D_CONV, conv_dim), par),
            pl.BlockSpec((None, 1, conv_dim), par),
            pl.BlockSpec((None, 1, M_DT_PAD), par),
            pl.BlockSpec((None, 1, M_DT_PAD), par),
            pl.BlockSpec((None, 1, d_inner), par),
            pl.BlockSpec((None, 1, d_inner), par),
        ],
        out_specs=pl.BlockSpec((lt, d), tok),
        scratch_shapes=[pltpu.VMEM((lt + 8, conv_dim), F32),
                        pltpu.VMEM((lt, d_inner), F32),
                        pltpu.VMEM((lt, d_inner), BF16),
                        pltpu.VMEM((M_GROUPS, M_D_STATE, M_GW), F32)],
        compiler_params=_params(("parallel", "arbitrary")),
        name="mamba_layer",
    )(h, norms3, norms3, w_in, w_out, conv_w, conv_b, dt_bias, a_log, d_skip, norm_w)


def _head_sum(x, bd_ref):
    d = x.shape[1]
    cols = [_dot(x[:, c * R_QUAD:(c + 1) * R_QUAD].astype(BF16), bd_ref[...]) for c in range(d // R_QUAD)]
    return jnp.concatenate(cols, axis=1)


def _rwkv_project(h_ref, hp_ref, g_ref, mix_ref, wrkv_ref, w1_ref, w2_ref, a1_ref, a2_ref, g1_ref, g2_ref,
                  w0_ref, a0_ref, kk_ref, ka_ref, bd_ref, vres, seq_start,
                  r_ref, lw_ref, k_ref, v_ref, a_ref, b_ref, gate_ref):
    gn = g_ref[...]
    u = _rmsnorm(h_ref[...], gn)
    u_before = _rmsnorm(hp_ref[7:8, :], gn)
    u_before = jnp.where(seq_start, jnp.zeros_like(u_before), u_before)
    row = lax.broadcasted_iota(jnp.int32, u.shape, 0)
    u_prev = jnp.where(row == 0, u_before, pltpu.roll(u, 1, 0))
    u_bf = u.astype(BF16)
    delta_bf = (u_prev - u).astype(BF16)

    def shift(i):
        return u_bf + delta_bf * mix_ref[i:i + 1, :].astype(BF16)

    r_ref[...] = _dot(shift(0), wrkv_ref[0])
    k = _dot(shift(2), wrkv_ref[1])
    xv = shift(3)
    v = _dot(xv, wrkv_ref[2])
    if vres is not None:
        v0_ref, v1_ref, v2_ref, vf_ref = vres
        mixv = _sigmoid(v0_ref[...] + _dot(_dot(xv, v1_ref[...]).astype(BF16), v2_ref[...]))
        v = v + (vf_ref[...] - v) * mixv
    v_ref[...] = v
    wl = w0_ref[...] + _dot(jnp.tanh(_dot(shift(1), w1_ref[...])).astype(BF16), w2_ref[...])
    lw_ref[...] = -jnp.exp(-_softplus(-wl) - 0.5)
    gate_ref[...] = _dot(_sigmoid(_dot(shift(5), g1_ref[...])).astype(BF16), g2_ref[...])
    alpha = _sigmoid(a0_ref[...] + _dot(_dot(shift(4), a1_ref[...]).astype(BF16), a2_ref[...]))
    kk = k * kk_ref[...]
    kk = kk / jnp.maximum(jnp.sqrt(_head_sum(kk * kk, bd_ref)), R_L2_EPS)
    k_ref[...] = k * (1.0 + (alpha - 1.0) * ka_ref[...])
    a_ref[...] = -kk
    b_ref[...] = kk * alpha


def _rec_masks():
    li = np.arange(R_CHUNK)[:, None]
    lj = np.arange(R_CHUNK)[None, :]
    masks = [lj < li, lj <= li, (lj < li) & (li // 8 == lj // 8)]
    for m in (8, 16, 32):
        masks.append((li // (2 * m) == lj // (2 * m)) & (li % (2 * m) >= m) & (lj % (2 * m) < m))
    return jnp.asarray(np.stack([np.tile(m, (1, R_QUAD // R_HEAD)) for m in masks]).astype(np.float32))


def _rwkv_recurrence(r_ref, lw_ref, k_ref, v_ref, a_ref, b_ref, masks_ref, bd_ref, y_ref,
                     st_ref, qp_ref, g_ref, ct_ref, pe_ref, *, n_chunks, unroll):
    lc = R_CHUNK
    n_quads = r_ref.shape[1] // R_QUAD
    ti = lax.broadcasted_iota(jnp.int32, (lc, lc), 0)
    tj = lax.broadcasted_iota(jnp.int32, (lc, lc), 1)
    tril = jnp.where(ti >= tj, 1.0, 0.0).astype(BF16)
    lane_head = lax.broadcasted_iota(jnp.int32, (lc, R_QUAD), 1) // R_HEAD

    def quad(x):
        x = x.astype(BF16)
        zero = jnp.zeros_like(x)
        return jnp.concatenate([jnp.where(lane_head == hd, x, zero) for hd in range(R_QUAD // R_HEAD)], axis=0)

    each = lambda f, *ls: [f(*xs) for xs in zip(*ls)]
    hdot = lambda xs, ys: each(lambda x, y: _dot(x.astype(BF16), quad(y)), xs, ys)
    add3 = lambda xs, ys, zs: each(lambda x, y, z: x + y + z, xs, ys, zs)

    def tri_inverse_offdiag(n_mats):
        d1 = [n * masks_ref[2] for n in n_mats]
        d2 = hdot(d1, d1)
        d4 = hdot(d2, d2)
        xa = add3(d1, d2, hdot(d1, d2))
        x = add3(xa, d4, hdot(xa, d4))
        for lvl in range(3):
            nm = [n * masks_ref[3 + lvl] for n in n_mats]
            w = each(lambda p, q: p + q, nm, hdot(nm, x))
            x = add3(x, w, hdot(x, w))
        return x

    def build(insts):
        sls = [(pl.ds(pl.multiple_of(ci * lc, lc), lc), slice(q * R_QUAD, (q + 1) * R_QUAD)) for ci, q in insts]
        lw = [lw_ref[sl] for sl in sls]
        logp = [_dot_exact_x(tril, x) for x in lw]
        logp_end = [x[lc - 1:lc, :] for x in logp]
        pinv = [jnp.exp(-x) for x in logp]
        to_end = each(lambda e, x: jnp.exp(e - x), logp_end, logp)
        a_hat = [a_ref[sl] * jnp.exp(x - w) for sl, x, w in zip(sls, logp, lw)]
        q_un = [r_ref[sl] * jnp.exp(x) for sl, x in zip(sls, logp)]
        kc = [k_ref[sl] for sl in sls]
        bc = [b_ref[sl] for sl in sls]
        vc = [v_ref[sl] for sl in sls]
        lhs = each(lambda a, q: jnp.concatenate([a, q], axis=0).astype(BF16), a_hat, q_un)
        rhs = each(lambda b, k, pi: jnp.concatenate([quad(b * pi), quad(k * pi)], axis=0), bc, kc, pinv)
        a4 = each(_dot_nt, lhs, rhs)
        strict = masks_ref[0]
        incl = masks_ref[1]
        n_mat = [x[0:lc, 0:R_QUAD] * strict for x in a4]
        a_ak = [x[0:lc, R_QUAD:] * strict for x in a4]
        a_qb = [x[lc:, 0:R_QUAD] * incl for x in a4]
        a_qk = [(x[lc:, R_QUAD:] * incl).astype(BF16) for x in a4]

        v_q = [quad(x) for x in vc]
        z = each(lambda a, v: _dot(a.astype(BF16), v), a_ak, v_q)
        x = tri_inverse_offdiag(n_mat)
        xaz = each(lambda xx, a, zz: _dot(xx.astype(BF16), jnp.concatenate([quad(a), quad(zz)], axis=1)),
                   x, a_hat, z)
        ap = each(lambda a, d: a + d[:, 0:R_QUAD], a_hat, xaz)
        u0 = each(lambda zz, d: zz + d[:, R_QUAD:], z, xaz)
        qy = each(lambda m, a, u: _dot(m.astype(BF16), jnp.concatenate([quad(a), quad(u)], axis=1)),
                  a_qb, ap, u0)
        y0 = each(_dot, a_qk, v_q)
        bt = each(lambda b, e: (b * e).astype(BF16), bc, to_end)
        kt = each(lambda k, e: (k * e).astype(BF16), kc, to_end)
        g = each(lambda a, b: _dot_tn(a.astype(BF16), b), ap, bt)
        ct = each(lambda u, v, b, k: _dot_tn(jnp.concatenate([u.astype(BF16), v.astype(BF16)], axis=0),
                                             jnp.concatenate([b, k], axis=0)), u0, vc, bt, kt)
        same_head = bd_ref[...] > 0
        for i, (ci, q) in enumerate(insts):
            qp_ref[sls[i]] = (q_un[i] + qy[i][:, 0:R_QUAD]).astype(BF16)
            y_ref[sls[i]] = qy[i][:, R_QUAD:] + y0[i]
            g_ref[ci, q] = jnp.where(same_head, g[i], 0.0).astype(BF16)
            ct_ref[ci, q] = jnp.where(same_head, ct[i], 0.0)
            pe_ref[ci, :, q * R_QUAD:(q + 1) * R_QUAD] = jnp.broadcast_to(jnp.exp(logp_end[i]), (8, R_QUAD))

    def phase_a(i, carry):
        build([(i * unroll + u, q) for u in range(unroll) for q in range(n_quads)])
        return carry

    lax.fori_loop(0, n_chunks // unroll, phase_a, 0)

    def phase_b(ci, carry):
        rows = pl.ds(pl.multiple_of(ci * lc, lc), lc)
        lanes = [slice(q * R_QUAD, (q + 1) * R_QUAD) for q in range(n_quads)]
        st = [st_ref[q] for q in range(n_quads)]
        st_bf = [s.astype(BF16) for s in st]
        sg = [_dot(st_bf[q], g_ref[ci, q]) for q in range(n_quads)]
        yq = [_dot_nt(qp_ref[rows, lanes[q]], st_bf[q]) for q in range(n_quads)]
        for q in range(n_quads):
            st_ref[q] = st[q] * pe_ref[ci, 0:1, lanes[q]] + sg[q] + ct_ref[ci, q]
            y_ref[rows, lanes[q]] += yq[q]
        return carry

    lax.fori_loop(0, n_chunks, phase_b, 0)


def _rwkv_finish(y_ref, r_ref, k_ref, v_ref, gate_ref, lw_ref, lb_ref, rk_ref, bd_ref, wo_ref,
                 h_ref, g_ref, o_ref):
    y = y_ref[...]
    inv_n = 1.0 / R_HEAD
    mu = jnp.concatenate([_dot_x_exact(y[:, c * R_QUAD:(c + 1) * R_QUAD], bd_ref[...])
                          for c in range(y.shape[1] // R_QUAD)], axis=1) * inv_n
    yc = y - mu
    var = _head_sum(yc * yc, bd_ref) * inv_n
    yn = yc * lax.rsqrt(var + R_LNX_EPS) * lw_ref[...] + lb_ref[...]
    yn = yn + _head_sum(r_ref[...] * k_ref[...] * rk_ref[...], bd_ref) * v_ref[...]
    t = _dot((yn * gate_ref[...]).astype(BF16), wo_ref[...])
    o_ref[...] = h_ref[...] + _rmsnorm(t, g_ref[...])


def _rwkv_layer_kernel(*refs, has_vres, emit_v, n_chunks, unroll):
    (h_ref, hp_ref, gpre_ref, gpost_ref, mix_ref, wrkv_ref, w1_ref, w2_ref, a1_ref, a2_ref, g1_ref, g2_ref,
     w0_ref, a0_ref, kk_ref, ka_ref, bd_ref, lnw_ref, lnb_ref, rk_ref, wo_ref, masks_ref) = refs[:22]
    pos = 22
    vres = None
    if has_vres:
        vres = refs[pos:pos + 4]
        pos += 4
    o_ref = refs[pos]
    pos += 1
    if emit_v:
        v_ref = refs[pos]
        pos += 1
    (r_s, lw_s, k_s, a_s, b_s, gate_s, y_s) = refs[pos:pos + 7]
    pos += 7
    if not emit_v:
        v_ref = refs[pos]
        pos += 1
    st_s, qp_s, g_s, ct_s, pe_s = refs[pos:pos + 5]

    seq_start = pl.program_id(1) == 0

    @pl.when(seq_start)
    def _():
        st_s[...] = jnp.zeros_like(st_s)

    _rwkv_project(h_ref, hp_ref, gpre_ref, mix_ref, wrkv_ref, w1_ref, w2_ref, a1_ref, a2_ref, g1_ref, g2_ref,
                  w0_ref, a0_ref, kk_ref, ka_ref, bd_ref, vres, seq_start,
                  r_s, lw_s, k_s, v_ref, a_s, b_s, gate_s)
    _rwkv_recurrence(r_s, lw_s, k_s, v_ref, a_s, b_s, masks_ref, bd_ref, y_s,
                     st_s, qp_s, g_s, ct_s, pe_s, n_chunks=n_chunks, unroll=unroll)
    _rwkv_finish(y_s, r_s, k_s, v_ref, gate_s, lnw_ref, lnb_ref, rk_ref, bd_ref, wo_ref, h_ref, gpost_ref, o_ref)


def _rwkv_layer(h, norms3, layer, j, p, v_first, emit_v, batch, seq, lt):
    t, d = h.shape
    nt = seq // lt
    nch = lt // R_CHUNK
    nq = d // R_QUAD
    tok = lambda b, i: (b * nt + i, 0)
    prev = lambda b, i: (jnp.maximum((b * seq + i * lt) // 8 - 1, 0), 0)
    par3 = lambda b, i: (j, 0, 0)
    row = lambda: pl.BlockSpec((None, 1, d), par3)
    lp = R_LORA_PAD
    has_vres = v_first is not None
    masks = _rec_masks()
    in_specs = [
        pl.BlockSpec((lt, d), tok),
        pl.BlockSpec((8, d), prev),
        pl.BlockSpec((None, 1, d), lambda b, i: (layer * N_NORMS + 2, 0, 0)),
        pl.BlockSpec((None, 1, d), lambda b, i: (layer * N_NORMS + 3, 0, 0)),
        pl.BlockSpec((None, 8, d), par3),
        _resident((None, 3, d, d), lambda b, i: (j, 0, 0, 0)),
        _resident((None, d, lp), par3), _resident((None, lp, d), par3),
        _resident((None, d, lp), par3), _resident((None, lp, d), par3),
        _resident((None, d, lp), par3), _resident((None, lp, d), par3),
        row(), row(), row(), row(),
        _resident((R_QUAD, R_QUAD), lambda b, i: (0, 0)),
        row(), row(), row(),
        _resident((None, d, d), par3),
        _resident(masks.shape, lambda b, i: (0, 0, 0)),
    ]
    args = [h, h, norms3, norms3, p["mix"], p["w_rkv"], p["w1"], p["w2"], p["a1"], p["a2"], p["g1"], p["g2"],
            p["w0"], p["a0"], p["k_k"], p["k_a"], p["bd"], p["lnx_w"], p["lnx_b"], p["r_k"], p["w_o"], masks]
    if has_vres:
        parv = lambda b, i: (j - 1, 0, 0)
        in_specs += [pl.BlockSpec((None, 1, d), parv), _resident((None, d, lp), parv),
                     _resident((None, lp, d), parv), pl.BlockSpec((lt, d), tok)]
        args += [p["v0"], p["v1"], p["v2"], v_first]
    n_out = 2 if emit_v else 1
    scratch = [pltpu.VMEM((lt, d), F32) for _ in range(7 if emit_v else 8)]
    scratch += [pltpu.VMEM((nq, R_QUAD, R_QUAD), F32),
                pltpu.VMEM((lt, d), BF16),
                pltpu.VMEM((nch, nq, R_QUAD, R_QUAD), BF16),
                pltpu.VMEM((nch, nq, R_QUAD, R_QUAD), F32),
                pltpu.VMEM((nch, 8, d), F32)]
    kern = functools.partial(_rwkv_layer_kernel, has_vres=has_vres, emit_v=emit_v, n_chunks=nch,
                             unroll=4 if nch % 4 == 0 else 1)
    outs = pl.pallas_call(
        kern,
        out_shape=tuple(jax.ShapeDtypeStruct((t, d), F32) for _ in range(n_out)),
        grid=(batch, nt),
        in_specs=in_specs,
        out_specs=tuple(pl.BlockSpec((lt, d), tok) for _ in range(n_out)),
        scratch_shapes=scratch,
        compiler_params=_params(("parallel", "arbitrary")),
        name="rwkv_layer",
    )(*args)
    return outs if emit_v else (outs[0], None)


def _pad_last(x, n):
    return jnp.pad(x, [(0, 0)] * (x.ndim - 1) + [(0, n - x.shape[-1])])


def _pad_axis(x, axis, n):
    pads = [(0, 0)] * x.ndim
    pads[axis] = (0, n - x.shape[axis])
    return jnp.pad(x, pads)


def _row(x):
    return x.reshape(x.shape[0], 1, -1)


def kernel(x, norms, ffn_w_in, ffn_w_out, m_in_proj, m_conv_w, m_conv_b, m_dt_bias, m_A_log, m_D, m_norm_w, m_out_proj, r_mix, r_w_rkv, r_w_o, r_w0, r_w1, r_w2, r_a0, r_a1, r_a2, r_v0, r_v1, r_v2, r_g1, r_g2, r_k_k, r_k_a, r_r_k, r_lnx_w, r_lnx_b):
    batch, seq, d = x.shape
    depth = norms.shape[0]
    t = batch * seq
    d_inner = m_out_proj.shape[1]
    conv_dim = m_conv_w.shape[2]
    m_heads = m_dt_bias.shape[1]
    assert m_heads <= M_DT_PAD and d_inner == M_GROUPS * M_GW and d % R_QUAD == 0
    assert conv_dim == d_inner + 2 * M_GROUPS * M_D_STATE

    tm_ffn = min(1024, t)
    lt_mamba = min(256, seq)
    lt_rwkv = min(256, seq)

    norms3 = _row(norms.reshape(depth * N_NORMS, d))
    w_in = ffn_w_in.astype(BF16)
    w_out = ffn_w_out.astype(BF16)
    gn = M_GROUPS * M_D_STATE

    def group_major(w, parts):
        lead, off, cols = w.shape[:-1], 0, []
        for width in parts:
            cols.append(w[..., off:off + width].reshape(lead + (M_GROUPS, width // M_GROUPS)))
            off += width
        return jnp.concatenate(cols, axis=-1).reshape(lead + (off,))

    n_zx = d_inner + conv_dim
    m_in_bf = m_in_proj.astype(BF16)
    m_w = jnp.concatenate([group_major(m_in_bf[:, :, :n_zx], (d_inner, d_inner, gn, gn)),
                           _pad_last(m_in_bf[:, :, n_zx:], M_DT_PAD)], axis=2)
    m_wo = m_out_proj.astype(BF16)
    m_cw = group_major(m_conv_w, (d_inner, gn, gn))
    m_cb = _row(group_major(m_conv_b, (d_inner, gn, gn)))
    m_dtb = _row(_pad_last(m_dt_bias, M_DT_PAD))
    m_al = _row(_pad_last(m_A_log, M_DT_PAD))
    m_dsk = _row(jnp.repeat(m_D, M_HEADDIM, axis=1))
    m_nw = _row(m_norm_w)

    lp = R_LORA_PAD
    blk = np.arange(R_QUAD) // R_HEAD
    rp = {
        "mix": _pad_axis(r_mix, 1, 8),
        "w_rkv": r_w_rkv.astype(BF16), "w_o": r_w_o.astype(BF16),
        "w1": _pad_last(r_w1, lp).astype(BF16), "w2": _pad_axis(r_w2, 1, lp).astype(BF16),
        "a1": _pad_last(r_a1, lp).astype(BF16), "a2": _pad_axis(r_a2, 1, lp).astype(BF16),
        "g1": _pad_last(r_g1, lp).astype(BF16), "g2": _pad_axis(r_g2, 1, lp).astype(BF16),
        "w0": _row(r_w0), "a0": _row(r_a0), "k_k": _row(r_k_k), "k_a": _row(r_k_a),
        "r_k": _row(r_r_k.reshape(r_r_k.shape[0], -1)), "lnx_w": _row(r_lnx_w), "lnx_b": _row(r_lnx_b),
        "bd": jnp.asarray((blk[:, None] == blk[None, :]).astype(np.float32), dtype=BF16),
    }
    if r_v0.shape[0] > 0:
        rp.update({"v0": _row(r_v0), "v1": _pad_last(r_v1, lp).astype(BF16),
                   "v2": _pad_axis(r_v2, 1, lp).astype(BF16)})

    h = x.reshape(t, d)
    v_first = None
    for i in range(depth):
        j = i // 2
        h = _ffn(h, norms3, w_in, w_out, i, 0, tm_ffn)
        if i % 2 == 0:
            h = _mamba_layer(h, norms3, m_w, m_wo, m_cw, m_cb, m_dtb, m_al, m_dsk, m_nw, i, j,
                             batch, seq, lt_mamba)
        else:
            emit_v = j == 0 and depth // 2 > 1
            h, v_layer = _rwkv_layer(h, norms3, i, j, rp, v_first if j > 0 else None, emit_v,
                                     batch, seq, lt_rwkv)
            if j == 0:
                v_first = v_layer
        h = _ffn(h, norms3, w_in, w_out, i, 1, tm_ffn)
    return h.reshape(batch, seq, d)
```
